```python
import jax, jax.numpy as jnp
from jax import lax
import numpy as np

D_MODEL = 1024
BATCH = 8
SEQ = 8192
DEPTH = 1

CHUNK = 64
RET_HEADS = 4
RET_DK = 128
RET_DV = 256
GLA_HEADS = 4
GLA_DK = 128
GLA_DV = 256
GLA_GATE_RANK = 16
GLA_GATE_TAU = 16.0
D_FF = 2816
ROPE_BASE = 10000.0
NORM_EPS = 1e-6
RET_QK = RET_HEADS * RET_DK
RET_V = RET_HEADS * RET_DV
GLA_QK = GLA_HEADS * GLA_DK
GLA_V = GLA_HEADS * GLA_DV
IN_SPLITS = (RET_QK, RET_QK, RET_V, RET_V, GLA_QK, GLA_QK, GLA_V, GLA_V, GLA_GATE_RANK, D_MODEL, D_MODEL)
D_IN = RET_QK * 2 + RET_V * 2 + GLA_QK * 2 + GLA_V * 2 + GLA_GATE_RANK + 2 * D_MODEL

kernel_name = "hybrid_retention_gla_macaron_block"


def _rmsnorm(x, w):
    xf = x.astype(jnp.float32)
    y = xf * lax.rsqrt(jnp.mean(xf * xf, axis=-1, keepdims=True) + NORM_EPS)
    return (y * w.astype(jnp.float32)).astype(x.dtype)


def _swiglu(x, w1, w3, w2):
    return (jax.nn.silu(x @ w1) * (x @ w3)) @ w2


def _split_columns(p):
    outs = []
    off = 0
    for width in IN_SPLITS:
        outs.append(p[..., off:off + width])
        off += width
    return outs


def _rotary(x, pos):
    half = x.shape[-1] // 2
    inv_freq = ROPE_BASE ** (-jnp.arange(half, dtype=jnp.float32) / half)
    ang = pos.astype(jnp.float32)[:, None] * inv_freq[None, :]
    cos = jnp.cos(ang)[:, None, :]
    sin = jnp.sin(ang)[:, None, :]
    x = x.astype(jnp.float32)
    x1, x2 = x[..., :half], x[..., half:]
    return jnp.concatenate([x1 * cos - x2 * sin, x2 * cos + x1 * sin], axis=-1)


def _to_chunks(t):
    b, s, h, d = t.shape
    return t.reshape(b, s // CHUNK, CHUNK, h, d).transpose(1, 0, 3, 2, 4)


def _from_chunks(t):
    n, b, h, c, d = t.shape
    return t.transpose(1, 0, 3, 2, 4).reshape(b, n * c, h, d)


def _retention_chunkwise(q, k, v):
    heads = jnp.arange(RET_HEADS, dtype=jnp.float32)
    log_gamma = jnp.log1p(-jnp.exp2(-5.0 - heads))
    idx = jnp.arange(CHUNK, dtype=jnp.float32)
    intra = jnp.exp(log_gamma[:, None, None] * jnp.abs(idx[:, None] - idx[None, :]))
    q_decay = jnp.exp(log_gamma[:, None] * (idx[None, :] + 1.0))[:, :, None]
    k_decay = jnp.exp(log_gamma[:, None] * (CHUNK - 1.0 - idx[None, :]))[:, :, None]
    chunk_decay = jnp.exp(log_gamma * CHUNK)[:, None, None]

    def step(state, inp):
        qc, kc, vc = inp
        scores = jnp.einsum('bhnd,bhmd->bhnm', qc, kc) * intra
        out = (jnp.einsum('bhnm,bhmv->bhnv', scores, vc)
               + jnp.einsum('bhnd,bhdv->bhnv', qc * q_decay, state))
        state = chunk_decay * state + jnp.einsum('bhmd,bhmv->bhdv', kc * k_decay, vc)
        return state, out

    b = q.shape[1]
    state0 = jnp.zeros((b, RET_HEADS, RET_DK, RET_DV), jnp.float32)
    _, out = lax.scan(step, state0, (q, k, v))
    return out


def _gla_chunkwise(q, k, v, log_alpha):
    def step(state, inp):
        qc, kc, vc, la = inp
        cum = jnp.cumsum(la, axis=-2)
        last = cum[..., -1:, :]
        pair = jnp.exp(-jnp.abs(cum[..., :, None, :] - cum[..., None, :, :]))
        scores = jnp.einsum('bhnmd,bhmd->bhnm', qc[..., :, None, :] * pair, kc)
        out = (jnp.einsum('bhnm,bhmv->bhnv', scores, vc)
               + jnp.einsum('bhnd,bhdv->bhnv', qc * jnp.exp(cum), state))
        state = (jnp.swapaxes(jnp.exp(last), -1, -2) * state
                 + jnp.einsum('bhmd,bhmv->bhdv', kc * jnp.exp(last - cum), vc))
        return state, out

    b = q.shape[1]
    state0 = jnp.zeros((b, GLA_HEADS, GLA_DK, GLA_DV), jnp.float32)
    _, out = lax.scan(step, state0, (q, k, v, log_alpha))
    return out


def setup_inputs(seed: int = 0) -> dict:
    key = jax.random.key(seed)
    ks = jax.random.split(key, 20)
    f32 = jnp.float32

    def nrm(k, shape, fan_in):
        return jax.random.normal(k, shape, f32) * (fan_in ** -0.5)

    def gain(k, shape):
        return 1.0 + 0.02 * jax.random.normal(k, shape, f32)

    return {
        "x": jax.random.normal(ks[0], (BATCH, SEQ, D_MODEL), f32),
        "norm_ffn1": gain(ks[1], (DEPTH, D_MODEL)),
        "ffn1_w1": nrm(ks[2], (DEPTH, D_MODEL, D_FF), D_MODEL),
        "ffn1_w3": nrm(ks[3], (DEPTH, D_MODEL, D_FF), D_MODEL),
        "ffn1_w2": nrm(ks[4], (DEPTH, D_FF, D_MODEL), D_FF),
        "norm_mix": gain(ks[5], (DEPTH, D_MODEL)),
        "w_in": nrm(ks[6], (DEPTH, D_MODEL, D_IN), D_MODEL),
        "gla_gate_w2": nrm(ks[7], (DEPTH, GLA_GATE_RANK, GLA_QK), GLA_GATE_RANK),
        "gla_gate_b": 0.1 * jax.random.normal(ks[8], (DEPTH, GLA_QK), f32),
        "gla_norm_w": gain(ks[9], (DEPTH, GLA_DV)),
        "w_branch_ret": nrm(ks[10], (DEPTH, RET_V, D_MODEL), RET_V),
        "w_branch_gla": nrm(ks[11], (DEPTH, GLA_V, D_MODEL), GLA_V),
        "w_out": nrm(ks[12], (DEPTH, D_MODEL, D_MODEL), D_MODEL),
        "norm_ffn2": gain(ks[13], (DEPTH, D_MODEL)),
        "ffn2_w1": nrm(ks[14], (DEPTH, D_MODEL, D_FF), D_MODEL),
        "ffn2_w3": nrm(ks[15], (DEPTH, D_MODEL, D_FF), D_MODEL),
        "ffn2_w2": nrm(ks[16], (DEPTH, D_FF, D_MODEL), D_FF),
        "norm_final": gain(ks[17], (D_MODEL,)),
    }


def reference(x, norm_ffn1, ffn1_w1, ffn1_w3, ffn1_w2, norm_mix, w_in, gla_gate_w2, gla_gate_b,
              gla_norm_w, w_branch_ret, w_branch_gla, w_out, norm_ffn2, ffn2_w1, ffn2_w3, ffn2_w2,
              norm_final):
    b, s, _ = x.shape
    pos = jnp.arange(s, dtype=jnp.int32)
    h = x
    for l in range(DEPTH):
        h = h + 0.5 * _swiglu(_rmsnorm(h, norm_ffn1[l]), ffn1_w1[l], ffn1_w3[l], ffn1_w2[l])

        u = _rmsnorm(h, norm_mix[l])
        qa, ka, va, ra, qb, kb, vb, rb, lr, ga, gb = _split_columns(u @ w_in[l])

        qa = _rotary(qa.reshape(b, s, RET_HEADS, RET_DK), pos) * (RET_DK ** -0.5)
        ka = _rotary(ka.reshape(b, s, RET_HEADS, RET_DK), pos)
        va = va.reshape(b, s, RET_HEADS, RET_DV).astype(jnp.float32)
        oa = _from_chunks(_retention_chunkwise(_to_chunks(qa), _to_chunks(ka), _to_chunks(va)))
        mu = jnp.mean(oa, axis=-1, keepdims=True)
        var = jnp.mean(jnp.square(oa - mu), axis=-1, keepdims=True)
        oa = ((oa - mu) * lax.rsqrt(var + NORM_EPS)).reshape(b, s, RET_V)
        oa = (jax.nn.silu(ra.astype(jnp.float32)) * oa).astype(h.dtype)
        y_ret = oa @ w_branch_ret[l]

        log_alpha = jax.nn.log_sigmoid((lr @ gla_gate_w2[l] + gla_gate_b[l]).astype(jnp.float32)) / GLA_GATE_TAU
        log_alpha = log_alpha.reshape(b, s, GLA_HEADS, GLA_DK)
        qb = qb.reshape(b, s, GLA_HEADS, GLA_DK).astype(jnp.float32) * (GLA_DK ** -0.5)
        kb = kb.reshape(b, s, GLA_HEADS, GLA_DK).astype(jnp.float32)
        vb = vb.reshape(b, s, GLA_HEADS, GLA_DV).astype(jnp.float32)
        ob = _from_chunks(_gla_chunkwise(_to_chunks(qb), _to_chunks(kb), _to_chunks(vb), _to_chunks(log_alpha)))
        ob = ob * lax.rsqrt(jnp.mean(ob * ob, axis=-1, keepdims=True) + NORM_EPS) * gla_norm_w[l].astype(jnp.float32)
        ob = (jax.nn.silu(rb.astype(jnp.float32)) * ob.reshape(b, s, GLA_V)).astype(h.dtype)
        y_gla = ob @ w_branch_gla[l]

        merged = jax.nn.sigmoid(ga) * y_ret + jax.nn.sigmoid(gb) * y_gla
        h = h + merged @ w_out[l]

        h = h + 0.5 * _swiglu(_rmsnorm(h, norm_ffn2[l]), ffn2_w1[l], ffn2_w3[l], ffn2_w2[l])
    return _rmsnorm(h, norm_final)
```

```python
import functools
import math

import numpy as np
import jax
import jax.numpy as jnp
from jax import lax
from jax.experimental import pallas as pl
from jax.experimental.pallas import tpu as pltpu

D_MODEL = 1024
CHUNK = 64
HEADS = 4
DK = 128
DV = 256
QK = HEADS * DK
VW = HEADS * DV
GATE_RANK = 16
GATE_TAU = 16.0
D_FF = 2816
ROPE_BASE = 10000.0
NORM_EPS = 1e-6
IN_SPLITS = (QK, QK, VW, VW, QK, QK, VW, VW, GATE_RANK, D_MODEL, D_MODEL)

LANES = 128
TM = 512
LBLK = 256
TB = 1024
FF_CHUNKS = ((0, 1024), (1024, 2048), (2048, D_FF))
VMEM_LIMIT = 52 * 1024 * 1024

F32 = jnp.float32
BF16 = jnp.bfloat16


def _const_spec(shape):
    zeros = (0,) * len(shape)
    return pl.BlockSpec(shape, lambda *_: zeros, pipeline_mode=pl.Buffered(1))


def _rms(x, w):
    ms = jnp.mean(x * x, axis=-1, keepdims=True)
    return x * lax.rsqrt(ms + NORM_EPS) * w


def _dot(a, b):
    return jnp.dot(a, b, preferred_element_type=F32)


def _dot_nt(a, b):
    return lax.dot_general(a, b, (((1,), (1,)), ((), ())), preferred_element_type=F32)


def _dot_tn(a, b):
    return lax.dot_general(a, b, (((0,), (0,)), ((), ())), preferred_element_type=F32)


def _sigmoid(x):
    return 1.0 / (1.0 + jnp.exp(-x))


def _ffn_kernel(x_ref, nw_ref, w1_ref, w3_ref, w2_ref, *rest, final_norm):
    o_ref = rest[-1]
    x = x_ref[...]
    xn = _rms(x, nw_ref[...]).astype(BF16)
    acc = None
    for a, b in FF_CHUNKS:
        a1 = _dot(xn, w1_ref[:, a:b])
        a3 = _dot(xn, w3_ref[:, a:b])
        g = (a1 * _sigmoid(a1) * a3).astype(BF16)
        p = _dot(g, w2_ref[a:b, :])
        acc = p if acc is None else acc + p
    h = x + 0.5 * acc
    if final_norm:
        h = _rms(h, rest[0][...])
    o_ref[...] = h


def _ffn(x, nw, w1, w3, w2, final_w=None):
    t = x.shape[0]
    tile = pl.BlockSpec((TM, D_MODEL), lambda i: (i, 0))
    in_specs = [tile, _const_spec((1, D_MODEL)), _const_spec((D_MODEL, D_FF)),
                _const_spec((D_MODEL, D_FF)), _const_spec((D_FF, D_MODEL))]
    args = [x, nw, w1, w3, w2]
    if final_w is not None:
        in_specs.append(_const_spec((1, D_MODEL)))
        args.append(final_w)
    return pl.pallas_call(
        functools.partial(_ffn_kernel, final_norm=final_w is not None),
        grid=(t // TM,),
        in_specs=in_specs,
        out_specs=tile,
        out_shape=jax.ShapeDtypeStruct((t, D_MODEL), F32),
        compiler_params=pltpu.CompilerParams(
            dimension_semantics=("parallel",), vmem_limit_bytes=VMEM_LIMIT),
        name="ffn_final" if final_w is not None else "ffn",
    )(*args)


def _proj_kernel(h_ref, nw_ref, wqa, wka, wva, wqb, wkb, wvb, wlr, w2h, w2l, gb_ref,
                 cos_ref, sin_ref, qa_o, ka_o, va_o, qb_o, kb_o, vb_o, la_o):
    u = _rms(h_ref[...], nw_ref[...]).astype(BF16)
    cos = cos_ref[...]
    sin = sin_ref[...]
    scale = DK ** -0.5

    def rotary(x, o_ref, mul):
        for h in range(HEADS):
            xh = x[:, h * DK:(h + 1) * DK]
            r = xh * cos + pltpu.roll(xh, DK // 2, axis=1) * sin
            if mul != 1.0:
                r = r * mul
            o_ref[:, h * DK:(h + 1) * DK] = r.astype(BF16)

    rotary(_dot(u, wqa[...]), qa_o, scale)
    rotary(_dot(u, wka[...]), ka_o, 1.0)
    va_o[...] = _dot(u, wva[...]).astype(BF16)
    qb_o[...] = (_dot(u, wqb[...]) * scale).astype(BF16)
    kb_o[...] = _dot(u, wkb[...]).astype(BF16)
    vb_o[...] = _dot(u, wvb[...]).astype(BF16)
    lr = _dot(u, wlr[...])
    lr_hi = lr.astype(BF16)
    lr_lo = (lr - lr_hi.astype(F32)).astype(BF16)
    z = _dot(lr_hi, w2h[...]) + _dot(lr_lo, w2h[...]) + _dot(lr_hi, w2l[...]) + gb_ref[...]
    log_sig = jnp.minimum(z, 0.0) - jnp.log(1.0 + jnp.exp(-jnp.abs(z)))
    la_o[...] = log_sig * (1.0 / GATE_TAU)


def _proj(h1, nw, wqa, wka, wva, wqb, wkb, wvb, wlr, w2h, w2l, gate_b, cos_t, sin_t, seq):
    t = h1.shape[0]
    tiles_per_seq = seq // TM

    def tile(w):
        return pl.BlockSpec((TM, w), lambda i: (i, 0))

    pos_tile = pl.BlockSpec((TM, DK), lambda i: (i % tiles_per_seq, 0))
    in_specs = [tile(D_MODEL), _const_spec((1, D_MODEL)),
                _const_spec((D_MODEL, QK)), _const_spec((D_MODEL, QK)), _const_spec((D_MODEL, VW)),
                _const_spec((D_MODEL, QK)), _const_spec((D_MODEL, QK)), _const_spec((D_MODEL, VW)),
                _const_spec((D_MODEL, LANES)), _const_spec((LANES, QK)), _const_spec((LANES, QK)),
                _const_spec((1, QK)), pos_tile, pos_tile]
    out_specs = [tile(QK), tile(QK), tile(VW), tile(QK), tile(QK), tile(VW), tile(QK)]
    out_shape = [jax.ShapeDtypeStruct((t, w), d) for w, d in
                 ((QK, BF16), (QK, BF16), (VW, BF16), (QK, BF16), (QK, BF16), (VW, BF16), (QK, F32))]
    return pl.pallas_call(
        _proj_kernel,
        grid=(t // TM,),
        in_specs=in_specs,
        out_specs=out_specs,
        out_shape=out_shape,
        compiler_params=pltpu.CompilerParams(
            dimension_semantics=("parallel",), vmem_limit_bytes=VMEM_LIMIT),
        name="proj",
    )(h1, nw, wqa, wka, wva, wqb, wkb, wvb, wlr, w2h, w2l, gate_b, cos_t, sin_t)


def _ret_tables():
    idx = np.arange(LBLK, dtype=np.float64)
    chunk = np.arange(LBLK) // CHUNK
    diff = idx[:, None] - idx[None, :]
    same = chunk[:, None] == chunk[None, :]
    earlier = chunk[None, :] < chunk[:, None]
    dmask, qdec, kdec, cdec = [], [], [], []
    for h in range(HEADS):
        lg = math.log1p(-2.0 ** (-5.0 - h))
        dmask.append(np.where(same, np.exp(lg * np.abs(diff)),
                              np.where(earlier, np.exp(lg * diff), 0.0)))
        qdec.append(np.broadcast_to(np.exp(lg * (idx + 1.0))[:, None], (LBLK, LANES)))
        kdec.append(np.broadcast_to(np.exp(lg * (LBLK - 1.0 - idx))[:, None], (LBLK, LANES)))
        cdec.append(math.exp(lg * LBLK))
    f = lambda a: jnp.asarray(np.stack(a), dtype=F32)
    return f(dmask), f(qdec), f(kdec), tuple(cdec)


def _ret_kernel(q_ref, k_ref, v_ref, dmask_ref, qdec_ref, kdec_ref, o_ref, state_ref, *, cdec):
    @pl.when(pl.program_id(1) == 0)
    def _():
        state_ref[...] = jnp.zeros_like(state_ref)

    def block(i, carry):
        rows = pl.ds(pl.multiple_of(i * LBLK, LBLK), LBLK)
        for h in range(HEADS):
            qk_cols = slice(h * DK, (h + 1) * DK)
            v_cols = slice(h * DV, (h + 1) * DV)
            q = q_ref[rows, qk_cols]
            k = k_ref[rows, qk_cols]
            v = v_ref[rows, v_cols]
            st = state_ref[h]
            s = _dot_nt(q, k) * dmask_ref[h]
            qdec = qdec_ref[h]
            inter = _dot(q, st.astype(BF16))
            inter = inter * jnp.concatenate([qdec, qdec], axis=1)
            o = _dot(s.astype(BF16), v) + inter
            kd = (k.astype(F32) * kdec_ref[h]).astype(BF16)
            state_ref[h] = cdec[h] * st + _dot_tn(kd, v)
            mu = jnp.mean(o, axis=-1, keepdims=True)
            oc = o - mu
            var = jnp.mean(oc * oc, axis=-1, keepdims=True)
            o_ref[rows, v_cols] = (oc * lax.rsqrt(var + NORM_EPS)).astype(BF16)
        return carry

    lax.fori_loop(0, TB // LBLK, block, 0)


def _retention(q, k, v, batch, seq):
    t = q.shape[0]
    nj = seq // TB
    dmask, qdec, kdec, cdec = _ret_tables()

    def tile(w):
        return pl.BlockSpec((TB, w), lambda b, j: (b * nj + j, 0))

    return pl.pallas_call(
        functools.partial(_ret_kernel, cdec=cdec),
        grid=(batch, nj),
        in_specs=[tile(QK), tile(QK), tile(VW), _const_spec((HEADS, LBLK, LBLK)),
                  _const_spec((HEADS, LBLK, LANES)), _const_spec((HEADS, LBLK, LANES))],
        out_specs=tile(VW),
        out_shape=jax.ShapeDtypeStruct((t, VW), BF16),
        scratch_shapes=[pltpu.VMEM((HEADS, DK, DV), F32)],
        compiler_params=pltpu.CompilerParams(
            dimension_semantics=("parallel", "arbitrary"), vmem_limit_bytes=VMEM_LIMIT),
        name="retention",
    )(q, k, v, dmask, qdec, kdec)


def _gla_tables():
    n = np.arange(LBLK)
    tril = (n[None, :] <= n[:, None]).astype(np.float32)
    x = n[:, None] ^ n[None, :]
    lvl = np.where(x == 0, -1, np.floor(np.log2(np.maximum(x, 1))).astype(np.int64))
    hidden = (n[None, :] // CHUNK) > (n[:, None] // CHUNK)
    lvl = np.where(hidden, -2, lvl)
    return jnp.asarray(tril, dtype=BF16), jnp.asarray(lvl, dtype=jnp.int32)


def _split_ref_rows(g, half):
    n = g.shape[0]
    size = 2 * half
    if size >= 8:
        g3 = g.reshape(n // size, size, g.shape[1])
        ref = jnp.broadcast_to(g3[:, half - 1:half, :], g3.shape)
        return ref.reshape(g.shape)
    pos = lax.broadcasted_iota(jnp.int32, g.shape, 0) & (size - 1)
    out = g
    for p in range(size):
        shift = p - (half - 1)
        if shift == 0:
            continue
        out = jnp.where(pos == p, pltpu.roll(g, shift % n, axis=0), out)
    return out


def _gla_kernel(q_ref, k_ref, v_ref, la_ref, tril_ref, lvl_ref, nw_ref, o_ref, state_ref):
    @pl.when(pl.program_id(1) == 0)
    def _():
        state_ref[...] = jnp.zeros_like(state_ref)

    n_levels = int(math.log2(LBLK))

    def block(i, carry):
        rows = pl.ds(pl.multiple_of(i * LBLK, LBLK), LBLK)
        la = la_ref[rows, :]
        la_hi = la.astype(BF16)
        rem = la - la_hi.astype(F32)
        la_mid = rem.astype(BF16)
        la_lo = (rem - la_mid.astype(F32)).astype(BF16)
        tril = tril_ref[...]
        g_all = _dot(tril, la_hi) + _dot(tril, la_mid) + _dot(tril, la_lo)
        lvl = lvl_ref[...]
        for h in range(HEADS):
            qk_cols = slice(h * DK, (h + 1) * DK)
            v_cols = slice(h * DV, (h + 1) * DV)
            g = g_all[:, qk_cols]
            qb = q_ref[rows, qk_cols]
            kb = k_ref[rows, qk_cols]
            v = v_ref[rows, v_cols]
            q = qb.astype(F32)
            k = kb.astype(F32)
            s = jnp.where(lvl == -1, _dot_nt(qb, kb), 0.0)
            for p in range(n_levels):
                e = jnp.exp(-jnp.abs(g - _split_ref_rows(g, 1 << p)))
                r = _dot_nt((q * e).astype(BF16), (k * e).astype(BF16))
                s = jnp.where(lvl == p, r, s)
            st = state_ref[h]
            inter = _dot_nt((q * jnp.exp(g)).astype(BF16), st.astype(BF16))
            o = _dot(s.astype(BF16), v) + inter
            g_last = g[LBLK - 1:LBLK, :]
            kd = (k * jnp.exp(g_last - g)).astype(BF16)
            state_ref[h] = st * jnp.exp(g_last) + _dot_tn(v, kd)
            ms = jnp.mean(o * o, axis=-1, keepdims=True)
            o_ref[rows, v_cols] = (o * lax.rsqrt(ms + NORM_EPS) * nw_ref[...]).astype(BF16)
        return carry

    lax.fori_loop(0, TB // LBLK, block, 0)


def _gla(q, k, v, la, norm_w, batch, seq):
    t = q.shape[0]
    nj = seq // TB
    tril, lvl = _gla_tables()

    def tile(w):
        return pl.BlockSpec((TB, w), lambda b, j: (b * nj + j, 0))

    return pl.pallas_call(
        _gla_kernel,
        grid=(batch, nj),
        in_specs=[tile(QK), tile(QK), tile(VW), tile(QK), _const_spec((LBLK, LBLK)),
                  _const_spec((LBLK, LBLK)), _const_spec((1, DV))],
        out_specs=tile(VW),
        out_shape=jax.ShapeDtypeStruct((t, VW), BF16),
        scratch_shapes=[pltpu.VMEM((HEADS, DV, DK), F32)],
        compiler_params=pltpu.CompilerParams(
            dimension_semantics=("parallel", "arbitrary"), vmem_limit_bytes=VMEM_LIMIT),
        name="gla",
    )(q, k, v, la, tril, lvl, norm_w)


def _merge_kernel(h_ref, oa_ref, ob_ref, nw_ref, wra, wrb, wga, wgb, wa, wb, wo, o_ref):
    h = h_ref[...]
    u = _rms(h, nw_ref[...]).astype(BF16)

    def branch(o_in, w_gate_out, w_branch, w_gate_merge):
        r = _dot(u, w_gate_out[...])
        og = (r * _sigmoid(r) * o_in[...].astype(F32)).astype(BF16)
        y = _dot(og, w_branch[...])
        return _sigmoid(_dot(u, w_gate_merge[...])) * y

    merged = branch(oa_ref, wra, wa, wga) + branch(ob_ref, wrb, wb, wgb)
    o_ref[...] = h + _dot(merged.astype(BF16), wo[...])


def _merge(h1, oa, ob, nw, wra, wrb, wga, wgb, wa, wb, wo):
    t = h1.shape[0]
    tile_f = pl.BlockSpec((TM, D_MODEL), lambda i: (i, 0))
    sq = _const_spec((D_MODEL, D_MODEL))
    return pl.pallas_call(
        _merge_kernel,
        grid=(t // TM,),
        in_specs=[tile_f, tile_f, tile_f, _const_spec((1, D_MODEL)), sq, sq, sq, sq, sq, sq, sq],
        out_specs=tile_f,
        out_shape=jax.ShapeDtypeStruct((t, D_MODEL), F32),
        compiler_params=pltpu.CompilerParams(
            dimension_semantics=("parallel",), vmem_limit_bytes=VMEM_LIMIT),
        name="merge",
    )(h1, oa, ob, nw, wra, wrb, wga, wgb, wa, wb, wo)


def _rope_tables(seq):
    half = DK // 2
    inv_freq = ROPE_BASE ** (-np.arange(half, dtype=np.float64) / half)
    ang = np.arange(seq, dtype=np.float64)[:, None] * inv_freq[None, :]
    cos = np.concatenate([np.cos(ang), np.cos(ang)], axis=1)
    sin = np.concatenate([-np.sin(ang), np.sin(ang)], axis=1)
    return jnp.asarray(cos, dtype=F32), jnp.asarray(sin, dtype=F32)


def kernel(x, norm_ffn1, ffn1_w1, ffn1_w3, ffn1_w2, norm_mix, w_in, gla_gate_w2, gla_gate_b, gla_norm_w,
           w_branch_ret, w_branch_gla, w_out, norm_ffn2, ffn2_w1, ffn2_w3, ffn2_w2, norm_final):
    batch, seq, _ = x.shape
    depth = norm_ffn1.shape[0]
    t = batch * seq
    h = x.reshape(t, D_MODEL)
    cos_t, sin_t = _rope_tables(seq)
    row = lambda a: a.reshape(1, -1).astype(F32)
    for l in range(depth):
        h = _ffn(h, row(norm_ffn1[l]), ffn1_w1[l].astype(BF16), ffn1_w3[l].astype(BF16),
                 ffn1_w2[l].astype(BF16))

        cols, off = [], 0
        for width in IN_SPLITS:
            cols.append(w_in[l][:, off:off + width])
            off += width
        wqa, wka, wva, wra, wqb, wkb, wvb, wrb, wlr, wga, wgb = cols
        wlr_p = jnp.pad(wlr, ((0, 0), (0, LANES - GATE_RANK))).astype(BF16)
        w2 = jnp.pad(gla_gate_w2[l], ((0, LANES - GATE_RANK), (0, 0)))
        w2h = w2.astype(BF16)
        w2l = (w2 - w2h.astype(F32)).astype(BF16)
        bf = lambda a: a.astype(BF16)
        qa, ka, va, qb, kb, vb, la = _proj(
            h, row(norm_mix[l]), bf(wqa), bf(wka), bf(wva), bf(wqb), bf(wkb), bf(wvb), wlr_p, w2h, w2l,
            row(gla_gate_b[l]), cos_t, sin_t, seq)
        oa = _retention(qa, ka, va, batch, seq)
        ob = _gla(qb, kb, vb, la, row(gla_norm_w[l]), batch, seq)
        h = _merge(h, oa, ob, row(norm_mix[l]), bf(wra), bf(wrb), bf(wga), bf(wgb),
                   bf(w_branch_ret[l]), bf(w_branch_gla[l]), bf(w_out[l]))
        last = l == depth - 1
        h = _ffn(h, row(norm_ffn2[l]), bf(ffn2_w1[l]), bf(ffn2_w3[l]), bf(ffn2_w2[l]),
                 final_w=row(norm_final) if last else None)
    return h.reshape(batch, seq, D_MODEL)
```

```python
import functools
import math

import numpy as np
import jax
import jax.numpy as jnp
from jax import lax
from jax.experimental import pallas as pl
from jax.experimental.pallas import tpu as pltpu

D_MODEL = 1024
CHUNK = 64
HEADS = 4
DK = 128
DV = 256
QK = HEADS * DK
VW = HEADS * DV
GATE_RANK = 16
GATE_TAU = 16.0
D_FF = 2816
ROPE_BASE = 10000.0
NORM_EPS = 1e-6
IN_SPLITS = (QK, QK, VW, VW, QK, QK, VW, VW, GATE_RANK, D_MODEL, D_MODEL)

LANES = 128
TM = 512
LBLK = 256
GRP = LBLK // 2
N_LEVELS = LBLK.bit_length() - 1
TB = 1024
FF_CHUNKS = ((0, 1024), (1024, 2048), (2048, D_FF))
VMEM_LIMIT = 52 * 1024 * 1024

F32 = jnp.float32
BF16 = jnp.bfloat16


def _const_spec(shape):
    zeros = (0,) * len(shape)
    return pl.BlockSpec(shape, lambda *_: zeros, pipeline_mode=pl.Buffered(1))


def _rms(x, w):
    ms = jnp.mean(x * x, axis=-1, keepdims=True)
    return x * lax.rsqrt(ms + NORM_EPS) * w


def _dot(a, b):
    return jnp.dot(a, b, preferred_element_type=F32)


def _dot_nt(a, b):
    return lax.dot_general(a, b, (((1,), (1,)), ((), ())), preferred_element_type=F32)


def _dot_tn(a, b):
    return lax.dot_general(a, b, (((0,), (0,)), ((), ())), preferred_element_type=F32)


def _sigmoid(x):
    return 1.0 / (1.0 + jnp.exp(-x))


def _ffn_kernel(x_ref, nw_ref, w1_ref, w3_ref, w2_ref, *rest, final_norm):
    o_ref = rest[-1]
    x = x_ref[...]
    xn = _rms(x, nw_ref[...]).astype(BF16)
    acc = None
    for a, b in FF_CHUNKS:
        a1 = _dot(xn, w1_ref[:, a:b])
        a3 = _dot(xn, w3_ref[:, a:b])
        g = (a1 * _sigmoid(a1) * a3).astype(BF16)
        p = _dot(g, w2_ref[a:b, :])
        acc = p if acc is None else acc + p
    h = x + 0.5 * acc
    if final_norm:
        h = _rms(h, rest[0][...])
    o_ref[...] = h


def _ffn(x, nw, w1, w3, w2, final_w=None):
    t = x.shape[0]
    tile = pl.BlockSpec((TM, D_MODEL), lambda i: (i, 0))
    in_specs = [tile, _const_spec((1, D_MODEL)), _const_spec((D_MODEL, D_FF)),
                _const_spec((D_MODEL, D_FF)), _const_spec((D_FF, D_MODEL))]
    args = [x, nw, w1, w3, w2]
    if final_w is not None:
        in_specs.append(_const_spec((1, D_MODEL)))
        args.append(final_w)
    return pl.pallas_call(
        functools.partial(_ffn_kernel, final_norm=final_w is not None),
        grid=(t // TM,),
        in_specs=in_specs,
        out_specs=tile,
        out_shape=jax.ShapeDtypeStruct((t, D_MODEL), F32),
        compiler_params=pltpu.CompilerParams(
            dimension_semantics=("parallel",), vmem_limit_bytes=VMEM_LIMIT),
        name="ffn_final" if final_w is not None else "ffn",
    )(*args)


def _proj_kernel(h_ref, nw_ref, wqa, wka, wva, wqb, wkb, wvb, wlr, w2h, w2l, gb_ref,
                 cos_ref, sin_ref, qa_o, ka_o, va_o, qb_o, kb_o, vb_o, la_o):
    u = _rms(h_ref[...], nw_ref[...]).astype(BF16)
    cos = cos_ref[...]
    sin = sin_ref[...]
    scale = DK ** -0.5

    def rotary(x, o_ref, mul):
        for h in range(HEADS):
            xh = x[:, h * DK:(h + 1) * DK]
            r = xh * cos + pltpu.roll(xh, DK // 2, axis=1) * sin
            if mul != 1.0:
                r = r * mul
            o_ref[:, h * DK:(h + 1) * DK] = r.astype(BF16)

    lr = _dot(u, wlr[...])
    lr_hi = lr.astype(BF16)
    lr_lo = (lr - lr_hi.astype(F32)).astype(BF16)
    z = _dot(lr_hi, w2h[...]) + _dot(lr_lo, w2h[...]) + _dot(lr_hi, w2l[...]) + gb_ref[...]
    rotary(_dot(u, wqa[...]), qa_o, scale)
    rotary(_dot(u, wka[...]), ka_o, 1.0)
    log_sig = jnp.minimum(z, 0.0) - jnp.log(1.0 + jnp.exp(-jnp.abs(z)))
    la_o[...] = log_sig * (1.0 / GATE_TAU)
    va_o[...] = _dot(u, wva[...]).astype(BF16)
    qb_o[...] = (_dot(u, wqb[...]) * scale).astype(BF16)
    kb_o[...] = _dot(u, wkb[...]).astype(BF16)
    vb_o[...] = _dot(u, wvb[...]).astype(BF16)


def _proj(h1, nw, wqa, wka, wva, wqb, wkb, wvb, wlr, w2h, w2l, gate_b, cos_t, sin_t, seq):
    t = h1.shape[0]
    tiles_per_seq = seq // TM

    def tile(w):
        return pl.BlockSpec((TM, w), lambda i: (i, 0))

    pos_tile = pl.BlockSpec((TM, DK), lambda i: (i % tiles_per_seq, 0))
    in_specs = [tile(D_MODEL), _const_spec((1, D_MODEL)),
                _const_spec((D_MODEL, QK)), _const_spec((D_MODEL, QK)), _const_spec((D_MODEL, VW)),
                _const_spec((D_MODEL, QK)), _const_spec((D_MODEL, QK)), _const_spec((D_MODEL, VW)),
                _const_spec((D_MODEL, LANES)), _const_spec((LANES, QK)), _const_spec((LANES, QK)),
                _const_spec((1, QK)), pos_tile, pos_tile]
    out_specs = [tile(QK), tile(QK), tile(VW), tile(QK), tile(QK), tile(VW), tile(QK)]
    out_shape = [jax.ShapeDtypeStruct((t, w), d) for w, d in
                 ((QK, BF16), (QK, BF16), (VW, BF16), (QK, BF16), (QK, BF16), (VW, BF16), (QK, F32))]
    return pl.pallas_call(
        _proj_kernel,
        grid=(t // TM,),
        in_specs=in_specs,
        out_specs=out_specs,
        out_shape=out_shape,
        compiler_params=pltpu.CompilerParams(
            dimension_semantics=("parallel",), vmem_limit_bytes=VMEM_LIMIT),
        name="proj",
    )(h1, nw, wqa, wka, wva, wqb, wkb, wvb, wlr, w2h, w2l, gate_b, cos_t, sin_t)


def _ret_tables():
    idx = np.arange(LBLK, dtype=np.float64)
    chunk = np.arange(LBLK) // CHUNK
    diff = idx[:, None] - idx[None, :]
    same = chunk[:, None] == chunk[None, :]
    earlier = chunk[None, :] < chunk[:, None]
    dmask, qdec, kdec, cdec = [], [], [], []
    for h in range(HEADS):
        lg = math.log1p(-2.0 ** (-5.0 - h))
        dmask.append(np.where(same, np.exp(lg * np.abs(diff)),
                              np.where(earlier, np.exp(lg * diff), 0.0)))
        qdec.append(np.broadcast_to(np.exp(lg * (idx + 1.0))[:, None], (LBLK, LANES)))
        kdec.append(np.broadcast_to(np.exp(lg * (LBLK - 1.0 - idx))[:, None], (LBLK, LANES)))
        cdec.append(math.exp(lg * LBLK))
    f = lambda a: jnp.asarray(np.stack(a), dtype=F32)
    return f(dmask), f(qdec), f(kdec), tuple(cdec)


def _ret_kernel(q_ref, k_ref, v_ref, dmask_ref, qdec_ref, kdec_ref, o_ref, state_ref, *, cdec):
    @pl.when(pl.program_id(1) == 0)
    def _():
        state_ref[...] = jnp.zeros_like(state_ref)

    def block(i, carry):
        rows = pl.ds(pl.multiple_of(i * LBLK, LBLK), LBLK)
        for h in range(HEADS):
            qk_cols = slice(h * DK, (h + 1) * DK)
            v_cols = slice(h * DV, (h + 1) * DV)
            q = q_ref[rows, qk_cols]
            k = k_ref[rows, qk_cols]
            v = v_ref[rows, v_cols]
            st = state_ref[h]
            s = _dot_nt(q, k) * dmask_ref[h]
            qdec = qdec_ref[h]
            inter = _dot(q, st.astype(BF16))
            inter = inter * jnp.concatenate([qdec, qdec], axis=1)
            o = _dot(s.astype(BF16), v) + inter
            kd = (k.astype(F32) * kdec_ref[h]).astype(BF16)
            state_ref[h] = cdec[h] * st + _dot_tn(kd, v)
            mu = jnp.mean(o, axis=-1, keepdims=True)
            oc = o - mu
            var = jnp.mean(oc * oc, axis=-1, keepdims=True)
            o_ref[rows, v_cols] = (oc * lax.rsqrt(var + NORM_EPS)).astype(BF16)
        return carry

    lax.fori_loop(0, TB // LBLK, block, 0)


def _retention(q, k, v, batch, seq):
    t = q.shape[0]
    nj = seq // TB
    dmask, qdec, kdec, cdec = _ret_tables()

    def tile(w):
        return pl.BlockSpec((TB, w), lambda b, j: (b * nj + j, 0))

    return pl.pallas_call(
        functools.partial(_ret_kernel, cdec=cdec),
        grid=(batch, nj),
        in_specs=[tile(QK), tile(QK), tile(VW), _const_spec((HEADS, LBLK, LBLK)),
                  _const_spec((HEADS, LBLK, LANES)), _const_spec((HEADS, LBLK, LANES))],
        out_specs=tile(VW),
        out_shape=jax.ShapeDtypeStruct((t, VW), BF16),
        scratch_shapes=[pltpu.VMEM((HEADS, DK, DV), F32)],
        compiler_params=pltpu.CompilerParams(
            dimension_semantics=("parallel", "arbitrary"), vmem_limit_bytes=VMEM_LIMIT),
        name="retention",
    )(q, k, v, dmask, qdec, kdec)


def _gla_tables():
    n = np.arange(LBLK)
    tril = (n[None, :] <= n[:, None]).astype(np.float32)
    m = np.arange(GRP)
    x = m[:, None] ^ m[None, :]
    lvl = np.where(x == 0, -1, np.floor(np.log2(np.maximum(x, 1))).astype(np.int64))
    hidden = (m[None, :] // CHUNK) > (m[:, None] // CHUNK)
    lvl = np.where(hidden, -2, lvl)
    second = ((n[None, :] >> np.arange(N_LEVELS)[:, None]) & 1).astype(np.float64)
    sgn = (2.0 * second - 1.0) * math.log2(math.e)
    sgn = np.broadcast_to(sgn[:, :, None], (N_LEVELS, LBLK, LANES))
    return (jnp.asarray(tril, dtype=BF16), jnp.asarray(lvl, dtype=jnp.int32),
            jnp.asarray(sgn, dtype=F32))


def _split_ref_rows(g, half):
    n = g.shape[0]
    size = 2 * half
    if size >= 8:
        g3 = g.reshape(n // size, size, g.shape[1])
        ref = jnp.broadcast_to(g3[:, half - 1:half, :], g3.shape)
        return ref.reshape(g.shape)
    pos = lax.broadcasted_iota(jnp.int32, g.shape, 0) & (size - 1)
    out = g
    for p in range(size):
        shift = p - (half - 1)
        if shift == 0:
            continue
        out = jnp.where(pos == p, pltpu.roll(g, shift % n, axis=0), out)
    return out


def _gla_kernel(q_ref, k_ref, v_ref, la_ref, tril_ref, lvl_ref, sgn_ref, nw_ref, o_ref, state_ref):
    @pl.when(pl.program_id(1) == 0)
    def _():
        state_ref[...] = jnp.zeros_like(state_ref)

    top, bot = slice(0, GRP), slice(GRP, LBLK)

    def block(i, carry):
        rows = pl.ds(pl.multiple_of(i * LBLK, LBLK), LBLK)
        la = la_ref[rows, :]
        la_hi = la.astype(BF16)
        rem = la - la_hi.astype(F32)
        la_mid = rem.astype(BF16)
        la_lo = (rem - la_mid.astype(F32)).astype(BF16)
        tril = tril_ref[...]
        g_all = _dot(tril, la_hi) + _dot(tril, la_mid) + _dot(tril, la_lo)
        lvl = lvl_ref[...]
        for h in range(HEADS):
            qk_cols = slice(h * DK, (h + 1) * DK)
            v_cols = slice(h * DV, (h + 1) * DV)
            g = g_all[:, qk_cols]
            qb = q_ref[rows, qk_cols]
            kb = k_ref[rows, qk_cols]
            v = v_ref[rows, v_cols]
            q = qb.astype(F32)
            k = kb.astype(F32)
            s_grp = [jnp.where(lvl == -1, _dot_nt(qb[r], kb[r]), 0.0) for r in (top, bot)]
            s_cross = None
            for p in range(N_LEVELS):
                e = jnp.exp2((g - _split_ref_rows(g, 1 << p)) * sgn_ref[p])
                qe = (q * e).astype(BF16)
                ke = (k * e).astype(BF16)
                if (2 << p) <= GRP:
                    s_grp = [jnp.where(lvl == p, _dot_nt(qe[r], ke[r]), s)
                             for r, s in zip((top, bot), s_grp)]
                else:
                    s_cross = _dot_nt(qe[bot], ke[top])
            st = state_ref[h]
            inter = _dot_nt(qb * jnp.exp(g).astype(BF16), st.astype(BF16))
            o_top = _dot(s_grp[0].astype(BF16), v[top])
            o_bot = _dot(jnp.concatenate([s_cross, s_grp[1]], axis=1).astype(BF16), v)
            o = jnp.concatenate([o_top, o_bot], axis=0) + inter
            g_last = g[LBLK - 1:LBLK, :]
            kd = kb * jnp.exp(g_last - g).astype(BF16)
            state_ref[h] = st * jnp.exp(g_last) + _dot_tn(v, kd)
            ms = jnp.mean(o * o, axis=-1, keepdims=True)
            o_ref[rows, v_cols] = (o * lax.rsqrt(ms + NORM_EPS) * nw_ref[...]).astype(BF16)
        return carry

    lax.fori_loop(0, TB // LBLK, block, 0)


def _gla(q, k, v, la, norm_w, batch, seq):
    t = q.shape[0]
    nj = seq // TB
    tril, lvl, sgn = _gla_tables()

    def tile(w):
        return pl.BlockSpec((TB, w), lambda b, j: (b * nj + j, 0))

    return pl.pallas_call(
        _gla_kernel,
        grid=(batch, nj),
        in_specs=[tile(QK), tile(QK), tile(VW), tile(QK), _const_spec((LBLK, LBLK)),
                  _const_spec((GRP, GRP)), _const_spec((N_LEVELS, LBLK, LANES)), _const_spec((1, DV))],
        out_specs=tile(VW),
        out_shape=jax.ShapeDtypeStruct((t, VW), BF16),
        scratch_shapes=[pltpu.VMEM((HEADS, DV, DK), F32)],
        compiler_params=pltpu.CompilerParams(
            dimension_semantics=("parallel", "arbitrary"), vmem_limit_bytes=VMEM_LIMIT),
        name="gla",
    )(q, k, v, la, tril, lvl, sgn, norm_w)


def _merge_kernel(h_ref, oa_ref, ob_ref, nw_ref, wra, wrb, wga, wgb, wa, wb, wo, o_ref):
    h = h_ref[...]
    u = _rms(h, nw_ref[...]).astype(BF16)

    def branch(o_in, w_gate_out, w_branch, w_gate_merge):
        r = _dot(u, w_gate_out[...])
        og = (r * _sigmoid(r) * o_in[...].astype(F32)).astype(BF16)
        y = _dot(og, w_branch[...])
        return _sigmoid(_dot(u, w_gate_merge[...])) * y

    merged = branch(oa_ref, wra, wa, wga) + branch(ob_ref, wrb, wb, wgb)
    o_ref[...] = h + _dot(merged.astype(BF16), wo[...])


def _merge(h1, oa, ob, nw, wra, wrb, wga, wgb, wa, wb, wo):
    t = h1.shape[0]
    tile_f = pl.BlockSpec((TM, D_MODEL), lambda i: (i, 0))
    sq = _const_spec((D_MODEL, D_MODEL))
    return pl.pallas_call(
        _merge_kernel,
        grid=(t // TM,),
        in_specs=[tile_f, tile_f, tile_f, _const_spec((1, D_MODEL)), sq, sq, sq, sq, sq, sq, sq],
        out_specs=tile_f,
        out_shape=jax.ShapeDtypeStruct((t, D_MODEL), F32),
        compiler_params=pltpu.CompilerParams(
            dimension_semantics=("parallel",), vmem_limit_bytes=VMEM_LIMIT),
        name="merge",
    )(h1, oa, ob, nw, wra, wrb, wga, wgb, wa, wb, wo)


def _rope_tables(seq):
    half = DK // 2
    inv_freq = ROPE_BASE ** (-np.arange(half, dtype=np.float64) / half)
    ang = np.arange(seq, dtype=np.float64)[:, None] * inv_freq[None, :]
    cos = np.concatenate([np.cos(ang), np.cos(ang)], axis=1)
    sin = np.concatenate([-np.sin(ang), np.sin(ang)], axis=1)
    return jnp.asarray(cos, dtype=F32), jnp.asarray(sin, dtype=F32)


def kernel(x, norm_ffn1, ffn1_w1, ffn1_w3, ffn1_w2, norm_mix, w_in, gla_gate_w2, gla_gate_b, gla_norm_w,
           w_branch_ret, w_branch_gla, w_out, norm_ffn2, ffn2_w1, ffn2_w3, ffn2_w2, norm_final):
    batch, seq, _ = x.shape
    depth = norm_ffn1.shape[0]
    t = batch * seq
    h = x.reshape(t, D_MODEL)
    cos_t, sin_t = _rope_tables(seq)
    row = lambda a: a.reshape(1, -1).astype(F32)
    for l in range(depth):
        h = _ffn(h, row(norm_ffn1[l]), ffn1_w1[l].astype(BF16), ffn1_w3[l].astype(BF16),
                 ffn1_w2[l].astype(BF16))

        cols, off = [], 0
        for width in IN_SPLITS:
            cols.append(w_in[l][:, off:off + width])
            off += width
        wqa, wka, wva, wra, wqb, wkb, wvb, wrb, wlr, wga, wgb = cols
        wlr_p = jnp.pad(wlr, ((0, 0), (0, LANES - GATE_RANK))).astype(BF16)
        w2 = jnp.pad(gla_gate_w2[l], ((0, LANES - GATE_RANK), (0, 0)))
        w2h = w2.astype(BF16)
        w2l = (w2 - w2h.astype(F32)).astype(BF16)
        bf = lambda a: a.astype(BF16)
        qa, ka, va, qb, kb, vb, la = _proj(
            h, row(norm_mix[l]), bf(wqa), bf(wka), bf(wva), bf(wqb), bf(wkb), bf(wvb), wlr_p, w2h, w2l,
            row(gla_gate_b[l]), cos_t, sin_t, seq)
        oa = _retention(qa, ka, va, batch, seq)
        ob = _gla(qb, kb, vb, la, row(gla_norm_w[l]), batch, seq)
        h = _merge(h, oa, ob, row(norm_mix[l]), bf(wra), bf(wrb), bf(wga), bf(wgb),
                   bf(w_branch_ret[l]), bf(w_branch_gla[l]), bf(w_out[l]))
        last = l == depth - 1
        h = _ffn(h, row(norm_ffn2[l]), bf(ffn2_w1[l]), bf(ffn2_w3[l]), bf(ffn2_w2[l]),
                 final_w=row(norm_final) if last else None)
    return h.reshape(batch, seq, D_MODEL)
```

```python
import functools
import math

import numpy as np
import jax
import jax.numpy as jnp
from jax import lax
from jax.experimental import pallas as pl
from jax.experimental.pallas import tpu as pltpu

D_MODEL = 1024
CHUNK = 64
HEADS = 4
DK = 128
DV = 256
QK = HEADS * DK
VW = HEADS * DV
GATE_RANK = 16
GATE_TAU = 16.0
D_FF = 2816
ROPE_BASE = 10000.0
NORM_EPS = 1e-6
IN_SPLITS = (QK, QK, VW, VW, QK, QK, VW, VW, GATE_RANK, D_MODEL, D_MODEL)

LANES = 128
TM = 512
LBLK = 256
GRP = LBLK // 2
N_LEVELS = LBLK.bit_length() - 1
TBLK = 512
FF_CHUNKS = ((0, 1024), (1024, 2048), (2048, D_FF))
VMEM_LIMIT = 52 * 1024 * 1024
MIX_VMEM_LIMIT = 60 * 1024 * 1024

F32 = jnp.float32
BF16 = jnp.bfloat16


def _const_spec(shape):
    zeros = (0,) * len(shape)
    return pl.BlockSpec(shape, lambda *_: zeros, pipeline_mode=pl.Buffered(1))


def _rms(x, w):
    ms = jnp.mean(x * x, axis=-1, keepdims=True)
    return x * lax.rsqrt(ms + NORM_EPS) * w


def _dot(a, b):
    return jnp.dot(a, b, preferred_element_type=F32)


def _dot_nt(a, b):
    return lax.dot_general(a, b, (((1,), (1,)), ((), ())), preferred_element_type=F32)


def _dot_tn(a, b):
    return lax.dot_general(a, b, (((0,), (0,)), ((), ())), preferred_element_type=F32)


def _sigmoid(x):
    return 1.0 / (1.0 + jnp.exp(-x))


def _ffn_kernel(x_ref, nw_ref, w1_ref, w3_ref, w2_ref, *rest, final_norm):
    o_ref = rest[-1]
    x = x_ref[...]
    xn = _rms(x, nw_ref[...]).astype(BF16)
    acc = None
    for a, b in FF_CHUNKS:
        a1 = _dot(xn, w1_ref[:, a:b])
        a3 = _dot(xn, w3_ref[:, a:b])
        g = (a1 * _sigmoid(a1) * a3).astype(BF16)
        p = _dot(g, w2_ref[a:b, :])
        acc = p if acc is None else acc + p
    h = x + 0.5 * acc
    if final_norm:
        h = _rms(h, rest[0][...])
    o_ref[...] = h


def _ffn(x, nw, w1, w3, w2, final_w=None):
    t = x.shape[0]
    tile = pl.BlockSpec((TM, D_MODEL), lambda i: (i, 0))
    in_specs = [tile, _const_spec((1, D_MODEL)), _const_spec((D_MODEL, D_FF)),
                _const_spec((D_MODEL, D_FF)), _const_spec((D_FF, D_MODEL))]
    args = [x, nw, w1, w3, w2]
    if final_w is not None:
        in_specs.append(_const_spec((1, D_MODEL)))
        args.append(final_w)
    return pl.pallas_call(
        functools.partial(_ffn_kernel, final_norm=final_w is not None),
        grid=(t // TM,),
        in_specs=in_specs,
        out_specs=tile,
        out_shape=jax.ShapeDtypeStruct((t, D_MODEL), F32),
        compiler_params=pltpu.CompilerParams(
            dimension_semantics=("parallel",), vmem_limit_bytes=VMEM_LIMIT),
        name="ffn_final" if final_w is not None else "ffn",
    )(*args)


def _proj_kernel(h_ref, nw_ref, wqa, wka, wva, wqb, wkb, wvb, wlr, w2h, w2l, gb_ref,
                 cos_ref, sin_ref, qa_o, ka_o, va_o, qb_o, kb_o, vb_o, la_o):
    u = _rms(h_ref[...], nw_ref[...]).astype(BF16)
    cos = cos_ref[...]
    sin = sin_ref[...]
    scale = DK ** -0.5

    def rotary(x, o_ref, mul):
        for h in range(HEADS):
            xh = x[:, h * DK:(h + 1) * DK]
            r = xh * cos + pltpu.roll(xh, DK // 2, axis=1) * sin
            if mul != 1.0:
                r = r * mul
            o_ref[:, h * DK:(h + 1) * DK] = r.astype(BF16)

    lr = _dot(u, wlr[...])
    lr_hi = lr.astype(BF16)
    lr_lo = (lr - lr_hi.astype(F32)).astype(BF16)
    z = _dot(lr_hi, w2h[...]) + _dot(lr_lo, w2h[...]) + _dot(lr_hi, w2l[...]) + gb_ref[...]
    rotary(_dot(u, wqa[...]), qa_o, scale)
    rotary(_dot(u, wka[...]), ka_o, 1.0)
    log_sig = jnp.minimum(z, 0.0) - jnp.log(1.0 + jnp.exp(-jnp.abs(z)))
    la_o[...] = log_sig * (1.0 / GATE_TAU)
    va_o[...] = _dot(u, wva[...]).astype(BF16)
    qb_o[...] = (_dot(u, wqb[...]) * scale).astype(BF16)
    kb_o[...] = _dot(u, wkb[...]).astype(BF16)
    vb_o[...] = _dot(u, wvb[...]).astype(BF16)


def _proj(h1, nw, wqa, wka, wva, wqb, wkb, wvb, wlr, w2h, w2l, gate_b, cos_t, sin_t, seq):
    t = h1.shape[0]
    tiles_per_seq = seq // TM

    def tile(w):
        return pl.BlockSpec((TM, w), lambda i: (i, 0))

    pos_tile = pl.BlockSpec((TM, DK), lambda i: (i % tiles_per_seq, 0))
    in_specs = [tile(D_MODEL), _const_spec((1, D_MODEL)),
                _const_spec((D_MODEL, QK)), _const_spec((D_MODEL, QK)), _const_spec((D_MODEL, VW)),
                _const_spec((D_MODEL, QK)), _const_spec((D_MODEL, QK)), _const_spec((D_MODEL, VW)),
                _const_spec((D_MODEL, LANES)), _const_spec((LANES, QK)), _const_spec((LANES, QK)),
                _const_spec((1, QK)), pos_tile, pos_tile]
    out_specs = [tile(QK), tile(QK), tile(VW), tile(QK), tile(QK), tile(VW), tile(QK)]
    out_shape = [jax.ShapeDtypeStruct((t, w), d) for w, d in
                 ((QK, BF16), (QK, BF16), (VW, BF16), (QK, BF16), (QK, BF16), (VW, BF16), (QK, F32))]
    return pl.pallas_call(
        _proj_kernel,
        grid=(t // TM,),
        in_specs=in_specs,
        out_specs=out_specs,
        out_shape=out_shape,
        compiler_params=pltpu.CompilerParams(
            dimension_semantics=("parallel",), vmem_limit_bytes=VMEM_LIMIT),
        name="proj",
    )(h1, nw, wqa, wka, wva, wqb, wkb, wvb, wlr, w2h, w2l, gate_b, cos_t, sin_t)


def _ret_tables():
    idx = np.arange(LBLK, dtype=np.float64)
    chunk = np.arange(LBLK) // CHUNK
    diff = idx[:, None] - idx[None, :]
    same = chunk[:, None] == chunk[None, :]
    earlier = chunk[None, :] < chunk[:, None]
    dmask, qdec, kdec, cdec = [], [], [], []
    for h in range(HEADS):
        lg = math.log1p(-2.0 ** (-5.0 - h))
        dmask.append(np.where(same, np.exp(lg * np.abs(diff)),
                              np.where(earlier, np.exp(lg * diff), 0.0)))
        qdec.append(np.broadcast_to(np.exp(lg * (idx + 1.0))[:, None], (LBLK, LANES)))
        kdec.append(np.broadcast_to(np.exp(lg * (LBLK - 1.0 - idx))[:, None], (LBLK, LANES)))
        cdec.append(math.exp(lg * LBLK))
    f = lambda a: jnp.asarray(np.stack(a), dtype=F32)
    return f(dmask), f(qdec), f(kdec), tuple(cdec)


def _gla_tables():
    n = np.arange(LBLK)
    tril = (n[None, :] <= n[:, None]).astype(np.float32)
    m = np.arange(GRP)
    x = m[:, None] ^ m[None, :]
    lvl = np.where(x == 0, -1, np.floor(np.log2(np.maximum(x, 1))).astype(np.int64))
    hidden = (m[None, :] // CHUNK) > (m[:, None] // CHUNK)
    lvl = np.where(hidden, -2, lvl)
    second = ((n[None, :] >> np.arange(N_LEVELS)[:, None]) & 1).astype(np.float64)
    sgn = (2.0 * second - 1.0) * math.log2(math.e)
    sgn = np.broadcast_to(sgn[:, :, None], (N_LEVELS, LBLK, LANES))
    return (jnp.asarray(tril, dtype=BF16), jnp.asarray(lvl, dtype=jnp.int32),
            jnp.asarray(sgn, dtype=F32))


def _split_ref_rows(g, half):
    n = g.shape[0]
    size = 2 * half
    if size >= 8:
        g3 = g.reshape(n // size, size, g.shape[1])
        ref = jnp.broadcast_to(g3[:, half - 1:half, :], g3.shape)
        return ref.reshape(g.shape)
    pos = lax.broadcasted_iota(jnp.int32, g.shape, 0) & (size - 1)
    out = g
    for p in range(size):
        shift = p - (half - 1)
        if shift == 0:
            continue
        out = jnp.where(pos == p, pltpu.roll(g, shift % n, axis=0), out)
    return out


RET_HEAD_COST = 150
GLA_PREP_COST = 250
GLA_SETUP_COST = 60
GLA_LEVEL_COST = 100
GLA_FINISH_COST = 250
MIXER_COST = (TBLK // LBLK) * (HEADS * RET_HEAD_COST + GLA_PREP_COST
                               + HEADS * (GLA_SETUP_COST + N_LEVELS * GLA_LEVEL_COST + GLA_FINISH_COST))
MERGE_NORM_COST = 300
MERGE_ROWS = 256
MERGE_PIECE_COLS = 512
MERGE_PIECE_COST = MERGE_ROWS * (D_MODEL // 256) * (MERGE_PIECE_COLS // 256) // 4
MERGE_COST = (TBLK // MERGE_ROWS) * (MERGE_NORM_COST
                                     + 7 * (D_MODEL // MERGE_PIECE_COLS) * MERGE_PIECE_COST)


def _interleave(gen_a, total_a, gen_b, total_b):
    gens, totals, done, at = [gen_a, gen_b], [total_a, total_b], [False, False], [0.0, 0.0]
    while not all(done):
        i = 0 if (not done[0] and (done[1] or at[0] <= at[1])) else 1
        try:
            at[i] += next(gens[i]) / totals[i]
        except StopIteration:
            done[i] = True


def _ret_block(q_ref, k_ref, v_ref, rows, dmask_ref, qdec_ref, kdec_ref, cdec, state_ref, o_ref, slot):
    for h in range(HEADS):
        qk_cols = slice(h * DK, (h + 1) * DK)
        v_cols = slice(h * DV, (h + 1) * DV)
        q = q_ref[rows, qk_cols]
        k = k_ref[rows, qk_cols]
        v = v_ref[rows, v_cols]
        st = state_ref[h]
        s = _dot_nt(q, k) * dmask_ref[h]
        qdec = qdec_ref[h]
        inter = _dot(q, st.astype(BF16)) * jnp.concatenate([qdec, qdec], axis=1)
        o = _dot(s.astype(BF16), v) + inter
        kd = (k.astype(F32) * kdec_ref[h]).astype(BF16)
        state_ref[h] = cdec[h] * st + _dot_tn(kd, v)
        mu = jnp.mean(o, axis=-1, keepdims=True)
        oc = o - mu
        var = jnp.mean(oc * oc, axis=-1, keepdims=True)
        o_ref[slot, rows, v_cols] = (oc * lax.rsqrt(var + NORM_EPS)).astype(BF16)
        yield RET_HEAD_COST


def _gla_block(q_ref, k_ref, v_ref, la_ref, rows, tril_ref, lvl_ref, sgn_ref, nw_ref, state_ref,
               o_ref, slot, pace):
    top, bot = slice(0, GRP), slice(GRP, LBLK)
    la = la_ref[rows, :]
    la_hi = la.astype(BF16)
    rem = la - la_hi.astype(F32)
    la_mid = rem.astype(BF16)
    la_lo = (rem - la_mid.astype(F32)).astype(BF16)
    tril = tril_ref[...]
    g_all = _dot(tril, la_hi) + _dot(tril, la_mid) + _dot(tril, la_lo)
    lvl = lvl_ref[...]
    yield GLA_PREP_COST
    for h in range(HEADS):
        qk_cols = slice(h * DK, (h + 1) * DK)
        v_cols = slice(h * DV, (h + 1) * DV)
        g = pace(g_all[:, qk_cols])
        qb = q_ref[rows, qk_cols]
        kb = k_ref[rows, qk_cols]
        v = v_ref[rows, v_cols]
        q = qb.astype(F32)
        k = kb.astype(F32)
        s_grp = [jnp.where(lvl == -1, _dot_nt(qb[r], kb[r]), 0.0) for r in (top, bot)]
        s_cross = None
        yield GLA_SETUP_COST
        for p in range(N_LEVELS):
            e = jnp.exp2((g - _split_ref_rows(g, 1 << p)) * sgn_ref[p])
            qe = (q * e).astype(BF16)
            ke = (k * e).astype(BF16)
            if (2 << p) <= GRP:
                s_grp = [jnp.where(lvl == p, _dot_nt(qe[r], ke[r]), s)
                         for r, s in zip((top, bot), s_grp)]
            else:
                s_cross = _dot_nt(qe[bot], ke[top])
            yield GLA_LEVEL_COST
        st = state_ref[h]
        inter = _dot_nt((q * jnp.exp(g)).astype(BF16), st.astype(BF16))
        o_top = _dot(s_grp[0].astype(BF16), v[top])
        o_bot = _dot(jnp.concatenate([s_cross, s_grp[1]], axis=1).astype(BF16), v)
        o = jnp.concatenate([o_top, o_bot], axis=0) + inter
        g_last = g[LBLK - 1:LBLK, :]
        kd = (k * jnp.exp(g_last - g)).astype(BF16)
        state_ref[h] = st * jnp.exp(g_last) + _dot_tn(v, kd)
        ms = jnp.mean(o * o, axis=-1, keepdims=True)
        o_ref[slot, rows, v_cols] = (o * lax.rsqrt(ms + NORM_EPS) * nw_ref[...]).astype(BF16)
        yield GLA_FINISH_COST


def _mix_kernel(qa_ref, ka_ref, va_ref, qb_ref, kb_ref, vb_ref, la_ref, h_ref, nw_ref,
                wra, wrb, wga, wgb, wa, wb, wo,
                dmask_ref, qdec_ref, kdec_ref, tril_ref, lvl_ref, sgn_ref, gnw_ref, zero_ref,
                o_ref, oa_scr, ob_scr, ret_state, gla_state, *, cdec, blocks_per_seq, n_blocks):
    s = pl.program_id(0)
    blk = jnp.minimum(s, n_blocks - 1)
    slot = lax.rem(s, 2)

    @pl.when(s == 0)
    def _():
        oa_scr[...] = jnp.zeros_like(oa_scr)
        ob_scr[...] = jnp.zeros_like(ob_scr)

    @pl.when(lax.rem(blk, blocks_per_seq) == 0)
    def _():
        ret_state[...] = jnp.zeros_like(ret_state)
        gla_state[...] = jnp.zeros_like(gla_state)

    newest = [None]

    def pace(x):
        if newest[0] is None:
            return x
        bits = lax.bitcast_convert_type(newest[0][0:8, 0:LANES], jnp.int32) & zero_ref[...]
        zero = lax.bitcast_convert_type(bits, F32)
        return (x.reshape(-1, 8, LANES) + zero[None]).reshape(x.shape)

    def mixers():
        for i in range(TBLK // LBLK):
            rows = slice(i * LBLK, (i + 1) * LBLK)
            yield from _ret_block(qa_ref, ka_ref, va_ref, rows, dmask_ref, qdec_ref, kdec_ref, cdec,
                                  ret_state, oa_scr, slot)
            yield from _gla_block(qb_ref, kb_ref, vb_ref, la_ref, rows, tril_ref, lvl_ref, sgn_ref,
                                  gnw_ref, gla_state, ob_scr, slot, pace)

    prev = 1 - slot
    pieces = [slice(c, c + MERGE_PIECE_COLS) for c in range(0, D_MODEL, MERGE_PIECE_COLS)]

    def dot_pieces(x, w_ref, out):
        for c in pieces:
            out.append(_dot(x, w_ref[:, c]))
            newest[0] = out[-1]
            yield MERGE_PIECE_COST

    def merge_slab(rows):
        u = _rms(h_ref[rows, :], nw_ref[...]).astype(BF16)
        yield MERGE_NORM_COST
        gated = []
        for o_scr, w_gate_out, w_branch, w_gate_merge in ((oa_scr, wra, wa, wga), (ob_scr, wrb, wb, wgb)):
            r, y, gm = [], [], []
            yield from dot_pieces(u, w_gate_out, r)
            og = jnp.concatenate(
                [(ri * _sigmoid(ri) * o_scr[prev, rows, c].astype(F32)).astype(BF16)
                 for ri, c in zip(r, pieces)], axis=1)
            yield from dot_pieces(og, w_branch, y)
            yield from dot_pieces(u, w_gate_merge, gm)
            gated.append([_sigmoid(gi) * yi for gi, yi in zip(gm, y)])
        merged = jnp.concatenate([(a + b).astype(BF16) for a, b in zip(*gated)], axis=1)
        out = []
        yield from dot_pieces(merged, wo, out)
        for c, oi in zip(pieces, out):
            o_ref[rows, c] = h_ref[rows, c] + oi

    def merge():
        for r0 in range(0, TBLK, MERGE_ROWS):
            yield from merge_slab(slice(r0, r0 + MERGE_ROWS))

    _interleave(mixers(), MIXER_COST, merge(), MERGE_COST)


def _mix(qa, ka, va, qb, kb, vb, la, h1, nw, wra, wrb, wga, wgb, wa, wb, wo, gla_nw, seq):
    t = h1.shape[0]
    n_blocks = t // TBLK
    dmask, qdec, kdec, cdec = _ret_tables()
    tril, lvl, sgn = _gla_tables()

    def cur(w):
        return pl.BlockSpec((TBLK, w), lambda s: (jnp.minimum(s, n_blocks - 1), 0))

    prev_tile = pl.BlockSpec((TBLK, D_MODEL), lambda s: (jnp.maximum(s - 1, 0), 0))
    sq = _const_spec((D_MODEL, D_MODEL))
    return pl.pallas_call(
        functools.partial(_mix_kernel, cdec=cdec, blocks_per_seq=seq // TBLK, n_blocks=n_blocks),
        grid=(n_blocks + 1,),
        in_specs=[cur(QK), cur(QK), cur(VW), cur(QK), cur(QK), cur(VW), cur(QK), prev_tile,
                  _const_spec((1, D_MODEL)), sq, sq, sq, sq, sq, sq, sq,
                  _const_spec((HEADS, LBLK, LBLK)), _const_spec((HEADS, LBLK, LANES)),
                  _const_spec((HEADS, LBLK, LANES)), _const_spec((LBLK, LBLK)), _const_spec((GRP, GRP)),
                  _const_spec((N_LEVELS, LBLK, LANES)), _const_spec((1, DV)), _const_spec((8, LANES))],
        out_specs=prev_tile,
        out_shape=jax.ShapeDtypeStruct((t, D_MODEL), F32),
        scratch_shapes=[pltpu.VMEM((2, TBLK, VW), BF16), pltpu.VMEM((2, TBLK, VW), BF16),
                        pltpu.VMEM((HEADS, DK, DV), F32), pltpu.VMEM((HEADS, DV, DK), F32)],
        compiler_params=pltpu.CompilerParams(
            dimension_semantics=("arbitrary",), vmem_limit_bytes=MIX_VMEM_LIMIT),
        name="mix",
    )(qa, ka, va, qb, kb, vb, la, h1, nw, wra, wrb, wga, wgb, wa, wb, wo,
      dmask, qdec, kdec, tril, lvl, sgn, gla_nw, jnp.zeros((8, LANES), jnp.int32))


def _rope_tables(seq):
    half = DK // 2
    inv_freq = ROPE_BASE ** (-np.arange(half, dtype=np.float64) / half)
    ang = np.arange(seq, dtype=np.float64)[:, None] * inv_freq[None, :]
    cos = np.concatenate([np.cos(ang), np.cos(ang)], axis=1)
    sin = np.concatenate([-np.sin(ang), np.sin(ang)], axis=1)
    return jnp.asarray(cos, dtype=F32), jnp.asarray(sin, dtype=F32)


def kernel(x, norm_ffn1, ffn1_w1, ffn1_w3, ffn1_w2, norm_mix, w_in, gla_gate_w2, gla_gate_b, gla_norm_w,
           w_branch_ret, w_branch_gla, w_out, norm_ffn2, ffn2_w1, ffn2_w3, ffn2_w2, norm_final):
    batch, seq, _ = x.shape
    depth = norm_ffn1.shape[0]
    t = batch * seq
    h = x.reshape(t, D_MODEL)
    cos_t, sin_t = _rope_tables(seq)
    row = lambda a: a.reshape(1, -1).astype(F32)
    for l in range(depth):
        h = _ffn(h, row(norm_ffn1[l]), ffn1_w1[l].astype(BF16), ffn1_w3[l].astype(BF16),
                 ffn1_w2[l].astype(BF16))

        cols, off = [], 0
        for width in IN_SPLITS:
            cols.append(w_in[l][:, off:off + width])
            off += width
        wqa, wka, wva, wra, wqb, wkb, wvb, wrb, wlr, wga, wgb = cols
        wlr_p = jnp.pad(wlr, ((0, 0), (0, LANES - GATE_RANK))).astype(BF16)
        w2 = jnp.pad(gla_gate_w2[l], ((0, LANES - GATE_RANK), (0, 0)))
        w2h = w2.astype(BF16)
        w2l = (w2 - w2h.astype(F32)).astype(BF16)
        bf = lambda a: a.astype(BF16)
        qa, ka, va, qb, kb, vb, la = _proj(
            h, row(norm_mix[l]), bf(wqa), bf(wka), bf(wva), bf(wqb), bf(wkb), bf(wvb), wlr_p, w2h, w2l,
            row(gla_gate_b[l]), cos_t, sin_t, seq)
        h = _mix(qa, ka, va, qb, kb, vb, la, h, row(norm_mix[l]), bf(wra), bf(wrb), bf(wga), bf(wgb),
                 bf(w_branch_ret[l]), bf(w_branch_gla[l]), bf(w_out[l]), row(gla_norm_w[l]), seq)
        last = l == depth - 1
        h = _ffn(h, row(norm_ffn2[l]), bf(ffn2_w1[l]), bf(ffn2_w3[l]), bf(ffn2_w2[l]),
                 final_w=row(norm_final) if last else None)
    return h.reshape(batch, seq, D_MODEL)
```

```python
import functools
import math

import numpy as np
import jax
import jax.numpy as jnp
from jax import lax
from jax.experimental import pallas as pl
from jax.experimental.pallas import tpu as pltpu

D_MODEL = 1024
CHUNK = 64
HEADS = 4
DK = 128
DV = 256
QK = HEADS * DK
VW = HEADS * DV
GATE_RANK = 16
GATE_TAU = 16.0
D_FF = 2816
ROPE_BASE = 10000.0
NORM_EPS = 1e-6
IN_SPLITS = (QK, QK, VW, VW, QK, QK, VW, VW, GATE_RANK, D_MODEL, D_MODEL)

LANES = 128
TM = 512
LBLK = 256
GRP = LBLK // 2
N_LEVELS = LBLK.bit_length() - 1
N_FINE = 2
TBLK = 512
FF_CHUNKS = ((0, 1024), (1024, 2048), (2048, D_FF))
VMEM_LIMIT = 52 * 1024 * 1024
MIX_VMEM_LIMIT = 60 * 1024 * 1024

F32 = jnp.float32
BF16 = jnp.bfloat16


def _const_spec(shape):
    zeros = (0,) * len(shape)
    return pl.BlockSpec(shape, lambda *_: zeros, pipeline_mode=pl.Buffered(1))


def _rms(x, w):
    ms = jnp.mean(x * x, axis=-1, keepdims=True)
    return x * lax.rsqrt(ms + NORM_EPS) * w


def _dot(a, b):
    return jnp.dot(a, b, preferred_element_type=F32)


def _dot_nt(a, b):
    return lax.dot_general(a, b, (((1,), (1,)), ((), ())), preferred_element_type=F32)


def _dot_tn(a, b):
    return lax.dot_general(a, b, (((0,), (0,)), ((), ())), preferred_element_type=F32)


def _sigmoid(x):
    return 1.0 / (1.0 + jnp.exp(-x))


def _ffn_kernel(x_ref, nw_ref, w1_ref, w3_ref, w2_ref, *rest, final_norm):
    o_ref = rest[-1]
    for r0 in range(0, TM, TM // 2):
        rows = slice(r0, r0 + TM // 2)
        x = x_ref[rows, :]
        xn = _rms(x, nw_ref[...]).astype(BF16)
        acc = None
        for a, b in FF_CHUNKS:
            a1 = _dot(xn, w1_ref[:, a:b])
            a3 = _dot(xn, w3_ref[:, a:b])
            g = (a1 * _sigmoid(a1) * a3).astype(BF16)
            p = _dot(g, w2_ref[a:b, :])
            acc = p if acc is None else acc + p
        h = x + 0.5 * acc
        if final_norm:
            h = _rms(h, rest[0][...])
        o_ref[rows, :] = h


def _ffn(x, nw, w1, w3, w2, final_w=None):
    t = x.shape[0]
    tile = pl.BlockSpec((TM, D_MODEL), lambda i: (i, 0))
    in_specs = [tile, _const_spec((1, D_MODEL)), _const_spec((D_MODEL, D_FF)),
                _const_spec((D_MODEL, D_FF)), _const_spec((D_FF, D_MODEL))]
    args = [x, nw, w1, w3, w2]
    if final_w is not None:
        in_specs.append(_const_spec((1, D_MODEL)))
        args.append(final_w)
    return pl.pallas_call(
        functools.partial(_ffn_kernel, final_norm=final_w is not None),
        grid=(t // TM,),
        in_specs=in_specs,
        out_specs=tile,
        out_shape=jax.ShapeDtypeStruct((t, D_MODEL), F32),
        compiler_params=pltpu.CompilerParams(
            dimension_semantics=("parallel",), vmem_limit_bytes=VMEM_LIMIT),
        name="ffn_final" if final_w is not None else "ffn",
    )(*args)


def _proj_kernel(h_ref, nw_ref, wqa, wka, wva, wqb, wkb, wvb, wlr, w2h, w2l, gb_ref,
                 cos_ref, sin_ref, qa_o, ka_o, va_o, qb_o, kb_o, vb_o, la_o):
    scale = DK ** -0.5
    for r0 in range(0, TM, TM // 2):
        rows = slice(r0, r0 + TM // 2)
        u = _rms(h_ref[rows, :], nw_ref[...]).astype(BF16)
        cos = cos_ref[rows, :]
        sin = sin_ref[rows, :]

        def rotary(x, o_ref, mul):
            for h in range(HEADS):
                xh = x[:, h * DK:(h + 1) * DK]
                r = xh * cos + pltpu.roll(xh, DK // 2, axis=1) * sin
                if mul != 1.0:
                    r = r * mul
                o_ref[rows, h * DK:(h + 1) * DK] = r.astype(BF16)

        lr = _dot(u, wlr[...])
        lr_hi = lr.astype(BF16)
        lr_lo = (lr - lr_hi.astype(F32)).astype(BF16)
        z = _dot(lr_hi, w2h[...]) + _dot(lr_lo, w2h[...]) + _dot(lr_hi, w2l[...]) + gb_ref[...]
        rotary(_dot(u, wqa[...]), qa_o, scale)
        rotary(_dot(u, wka[...]), ka_o, 1.0)
        log_sig = jnp.minimum(z, 0.0) - jnp.log(1.0 + jnp.exp(-jnp.abs(z)))
        la_o[rows, :] = log_sig * (1.0 / GATE_TAU)
        va_o[rows, :] = _dot(u, wva[...]).astype(BF16)
        qb_o[rows, :] = (_dot(u, wqb[...]) * scale).astype(BF16)
        kb_o[rows, :] = _dot(u, wkb[...]).astype(BF16)
        vb_o[rows, :] = _dot(u, wvb[...]).astype(BF16)


def _proj(h1, nw, wqa, wka, wva, wqb, wkb, wvb, wlr, w2h, w2l, gate_b, cos_t, sin_t, seq):
    t = h1.shape[0]
    tiles_per_seq = seq // TM

    def tile(w):
        return pl.BlockSpec((TM, w), lambda i: (i, 0))

    pos_tile = pl.BlockSpec((TM, DK), lambda i: (i % tiles_per_seq, 0))
    in_specs = [tile(D_MODEL), _const_spec((1, D_MODEL)),
                _const_spec((D_MODEL, QK)), _const_spec((D_MODEL, QK)), _const_spec((D_MODEL, VW)),
                _const_spec((D_MODEL, QK)), _const_spec((D_MODEL, QK)), _const_spec((D_MODEL, VW)),
                _const_spec((D_MODEL, LANES)), _const_spec((LANES, QK)), _const_spec((LANES, QK)),
                _const_spec((1, QK)), pos_tile, pos_tile]
    out_specs = [tile(QK), tile(QK), tile(VW), tile(QK), tile(QK), tile(VW), tile(QK)]
    out_shape = [jax.ShapeDtypeStruct((t, w), d) for w, d in
                 ((QK, BF16), (QK, BF16), (VW, BF16), (QK, BF16), (QK, BF16), (VW, BF16), (QK, F32))]
    return pl.pallas_call(
        _proj_kernel,
        grid=(t // TM,),
        in_specs=in_specs,
        out_specs=out_specs,
        out_shape=out_shape,
        compiler_params=pltpu.CompilerParams(
            dimension_semantics=("parallel",), vmem_limit_bytes=VMEM_LIMIT),
        name="proj",
    )(h1, nw, wqa, wka, wva, wqb, wkb, wvb, wlr, w2h, w2l, gate_b, cos_t, sin_t)


def _ret_tables():
    idx = np.arange(LBLK, dtype=np.float64)
    chunk = np.arange(LBLK) // CHUNK
    diff = idx[:, None] - idx[None, :]
    same = chunk[:, None] == chunk[None, :]
    earlier = chunk[None, :] < chunk[:, None]
    dmask, qdec, kdec, cdec = [], [], [], []
    for h in range(HEADS):
        lg = math.log1p(-2.0 ** (-5.0 - h))
        dmask.append(np.where(same, np.exp(lg * np.abs(diff)),
                              np.where(earlier, np.exp(lg * diff), 0.0)))
        qdec.append(np.broadcast_to(np.exp(lg * (idx + 1.0))[:, None], (LBLK, LANES)))
        kdec.append(np.broadcast_to(np.exp(lg * (LBLK - 1.0 - idx))[:, None], (LBLK, LANES)))
        cdec.append(math.exp(lg * LBLK))
    f = lambda a: jnp.asarray(np.stack(a), dtype=F32)
    return f(dmask), f(qdec), f(kdec), tuple(cdec)


def _gla_tables():
    n = np.arange(LBLK)
    mats = [n[None, :] <= n[:, None]]
    for p in range(N_FINE):
        ref = (n >> (p + 1) << (p + 1)) + (1 << p) - 1
        lo, hi = np.minimum(n, ref), np.maximum(n, ref)
        mats.append((n[None, :] > lo[:, None]) & (n[None, :] <= hi[:, None]))
    cum = np.concatenate(mats, axis=0).astype(np.float32)
    cum = np.concatenate([cum, cum], axis=1)
    m = np.arange(GRP)
    x = m[:, None] ^ m[None, :]
    lvl = np.where(x == 0, -1, np.floor(np.log2(np.maximum(x, 1))).astype(np.int64))
    hidden = (m[None, :] // CHUNK) > (m[:, None] // CHUNK)
    lvl = np.where(hidden, -2, lvl)
    second = ((n[None, :] >> np.arange(N_FINE, N_LEVELS)[:, None]) & 1).astype(np.float64)
    sgn = (2.0 * second - 1.0) * math.log2(math.e)
    sgn = np.broadcast_to(sgn[:, :, None], (N_LEVELS - N_FINE, LBLK, LANES))
    return (jnp.asarray(cum, dtype=BF16), jnp.asarray(lvl, dtype=jnp.int32),
            jnp.asarray(sgn, dtype=F32))


def _split_ref_rows(g, half):
    n = g.shape[0]
    size = 2 * half
    g3 = g.reshape(n // size, size, g.shape[1])
    ref = jnp.broadcast_to(g3[:, half - 1:half, :], g3.shape)
    return ref.reshape(g.shape)


RET_HEAD_COST = 150
GLA_PREP_COST = 250
GLA_SETUP_COST = 60
GLA_LEVEL_COST = 100
GLA_FINISH_COST = 250
MIXER_COST = (TBLK // LBLK) * (HEADS * RET_HEAD_COST + GLA_PREP_COST
                               + HEADS * (GLA_SETUP_COST + N_LEVELS * GLA_LEVEL_COST + GLA_FINISH_COST))
MERGE_NORM_COST = 300
MERGE_ROWS = 256
MERGE_PIECE_COLS = 512
MERGE_PIECE_COST = MERGE_ROWS * (D_MODEL // 256) * (MERGE_PIECE_COLS // 256) // 4
MERGE_COST = (TBLK // MERGE_ROWS) * (MERGE_NORM_COST
                                     + 7 * (D_MODEL // MERGE_PIECE_COLS) * MERGE_PIECE_COST)


def _interleave(gen_a, total_a, gen_b, total_b):
    gens, totals, done, at = [gen_a, gen_b], [total_a, total_b], [False, False], [0.0, 0.0]
    while not all(done):
        i = 0 if (not done[0] and (done[1] or at[0] <= at[1])) else 1
        try:
            at[i] += next(gens[i]) / totals[i]
        except StopIteration:
            done[i] = True


def _ret_block(q_ref, k_ref, v_ref, rows, dmask_ref, qdec_ref, kdec_ref, cdec, state_ref, o_ref, slot):
    for h in range(HEADS):
        qk_cols = slice(h * DK, (h + 1) * DK)
        v_cols = slice(h * DV, (h + 1) * DV)
        q = q_ref[rows, qk_cols]
        k = k_ref[rows, qk_cols]
        v = v_ref[rows, v_cols]
        st = state_ref[h]
        s = _dot_nt(q, k) * dmask_ref[h]
        qdec = qdec_ref[h]
        inter = _dot(q, st.astype(BF16)) * jnp.concatenate([qdec, qdec], axis=1)
        o = _dot(s.astype(BF16), v) + inter
        kd = (k.astype(F32) * kdec_ref[h]).astype(BF16)
        state_ref[h] = cdec[h] * st + _dot_tn(kd, v)
        mu = jnp.mean(o, axis=-1, keepdims=True)
        oc = o - mu
        var = jnp.mean(oc * oc, axis=-1, keepdims=True)
        o_ref[slot, rows, v_cols] = (oc * lax.rsqrt(var + NORM_EPS)).astype(BF16)
        yield RET_HEAD_COST


def _gla_block(q_ref, k_ref, v_ref, la_ref, rows, cum_ref, lvl_ref, sgn_ref, nw_ref, state_ref,
               o_ref, slot, pace):
    top, bot = slice(0, GRP), slice(GRP, LBLK)
    la = la_ref[rows, :]
    la_hi = la.astype(BF16)
    la_lo = (la - la_hi.astype(F32)).astype(BF16)
    cum = _dot(cum_ref[...], jnp.concatenate([la_hi, la_lo], axis=0))
    g_all = cum[0:LBLK]
    lvl = lvl_ref[...]
    yield GLA_PREP_COST
    for h in range(HEADS):
        qk_cols = slice(h * DK, (h + 1) * DK)
        v_cols = slice(h * DV, (h + 1) * DV)
        g = pace(g_all[:, qk_cols])
        qb = q_ref[rows, qk_cols]
        kb = k_ref[rows, qk_cols]
        v = v_ref[rows, v_cols]
        q = qb.astype(F32)
        k = kb.astype(F32)
        s_grp = [jnp.where(lvl == -1, _dot_nt(qb[r], kb[r]), 0.0) for r in (top, bot)]
        s_cross = None
        yield GLA_SETUP_COST
        for p in range(N_LEVELS):
            if p < N_FINE:
                e = jnp.exp(cum[(p + 1) * LBLK:(p + 2) * LBLK, qk_cols])
            else:
                e = jnp.exp2((g - _split_ref_rows(g, 1 << p)) * sgn_ref[p - N_FINE])
            qe = (q * e).astype(BF16)
            ke = (k * e).astype(BF16)
            if (2 << p) <= GRP:
                s_grp = [jnp.where(lvl == p, _dot_nt(qe[r], ke[r]), s)
                         for r, s in zip((top, bot), s_grp)]
            else:
                s_cross = _dot_nt(qe[bot], ke[top])
            yield GLA_LEVEL_COST
        st = state_ref[h]
        inter = _dot_nt((q * jnp.exp(g)).astype(BF16), st.astype(BF16))
        o_top = _dot(s_grp[0].astype(BF16), v[top])
        o_bot = _dot(jnp.concatenate([s_cross, s_grp[1]], axis=1).astype(BF16), v)
        o = jnp.concatenate([o_top, o_bot], axis=0) + inter
        g_last = g[LBLK - 1:LBLK, :]
        kd = (k * jnp.exp(g_last - g)).astype(BF16)
        state_ref[h] = st * jnp.exp(g_last) + _dot_tn(v, kd)
        ms = jnp.mean(o * o, axis=-1, keepdims=True)
        o_ref[slot, rows, v_cols] = (o * lax.rsqrt(ms + NORM_EPS) * nw_ref[...]).astype(BF16)
        yield GLA_FINISH_COST


def _mix_kernel(qa_ref, ka_ref, va_ref, qb_ref, kb_ref, vb_ref, la_ref, h_ref, nw_ref,
                wra, wrb, wga, wgb, wa, wb, wo,
                dmask_ref, qdec_ref, kdec_ref, cum_ref, lvl_ref, sgn_ref, gnw_ref, zero_ref,
                o_ref, oa_scr, ob_scr, ret_state, gla_state, *, cdec, blocks_per_seq, n_blocks):
    s = pl.program_id(0)
    blk = jnp.minimum(s, n_blocks - 1)
    slot = lax.rem(s, 2)

    @pl.when(s == 0)
    def _():
        oa_scr[...] = jnp.zeros_like(oa_scr)
        ob_scr[...] = jnp.zeros_like(ob_scr)

    @pl.when(lax.rem(blk, blocks_per_seq) == 0)
    def _():
        ret_state[...] = jnp.zeros_like(ret_state)
        gla_state[...] = jnp.zeros_like(gla_state)

    newest = [None]

    def pace(x):
        if newest[0] is None:
            return x
        bits = lax.bitcast_convert_type(newest[0][0:8, 0:LANES], jnp.int32) & zero_ref[...]
        zero = lax.bitcast_convert_type(bits, F32)
        return (x.reshape(-1, 8, LANES) + zero[None]).reshape(x.shape)

    def mixers():
        for i in range(TBLK // LBLK):
            rows = slice(i * LBLK, (i + 1) * LBLK)
            yield from _ret_block(qa_ref, ka_ref, va_ref, rows, dmask_ref, qdec_ref, kdec_ref, cdec,
                                  ret_state, oa_scr, slot)
            yield from _gla_block(qb_ref, kb_ref, vb_ref, la_ref, rows, cum_ref, lvl_ref, sgn_ref,
                                  gnw_ref, gla_state, ob_scr, slot, pace)

    prev = 1 - slot
    pieces = [slice(c, c + MERGE_PIECE_COLS) for c in range(0, D_MODEL, MERGE_PIECE_COLS)]

    def dot_pieces(x, w_ref, out):
        for c in pieces:
            out.append(_dot(x, w_ref[:, c]))
            newest[0] = out[-1]
            yield MERGE_PIECE_COST

    def merge_slab(rows):
        u = _rms(h_ref[rows, :], nw_ref[...]).astype(BF16)
        yield MERGE_NORM_COST
        gated = []
        for o_scr, w_gate_out, w_branch, w_gate_merge in ((oa_scr, wra, wa, wga), (ob_scr, wrb, wb, wgb)):
            r, y, gm = [], [], []
            yield from dot_pieces(u, w_gate_out, r)
            og = jnp.concatenate(
                [(ri * _sigmoid(ri) * o_scr[prev, rows, c].astype(F32)).astype(BF16)
                 for ri, c in zip(r, pieces)], axis=1)
            yield from dot_pieces(og, w_branch, y)
            yield from dot_pieces(u, w_gate_merge, gm)
            gated.append([_sigmoid(gi) * yi for gi, yi in zip(gm, y)])
        merged = jnp.concatenate([(a + b).astype(BF16) for a, b in zip(*gated)], axis=1)
        out = []
        yield from dot_pieces(merged, wo, out)
        for c, oi in zip(pieces, out):
            o_ref[rows, c] = h_ref[rows, c] + oi

    def merge():
        for r0 in range(0, TBLK, MERGE_ROWS):
            yield from merge_slab(slice(r0, r0 + MERGE_ROWS))

    _interleave(mixers(), MIXER_COST, merge(), MERGE_COST)


def _mix(qa, ka, va, qb, kb, vb, la, h1, nw, wra, wrb, wga, wgb, wa, wb, wo, gla_nw, seq):
    t = h1.shape[0]
    n_blocks = t // TBLK
    dmask, qdec, kdec, cdec = _ret_tables()
    cum, lvl, sgn = _gla_tables()

    def cur(w):
        return pl.BlockSpec((TBLK, w), lambda s: (jnp.minimum(s, n_blocks - 1), 0))

    prev_tile = pl.BlockSpec((TBLK, D_MODEL), lambda s: (jnp.maximum(s - 1, 0), 0))
    sq = _const_spec((D_MODEL, D_MODEL))
    return pl.pallas_call(
        functools.partial(_mix_kernel, cdec=cdec, blocks_per_seq=seq // TBLK, n_blocks=n_blocks),
        grid=(n_blocks + 1,),
        in_specs=[cur(QK), cur(QK), cur(VW), cur(QK), cur(QK), cur(VW), cur(QK), prev_tile,
                  _const_spec((1, D_MODEL)), sq, sq, sq, sq, sq, sq, sq,
                  _const_spec((HEADS, LBLK, LBLK)), _const_spec((HEADS, LBLK, LANES)),
                  _const_spec((HEADS, LBLK, LANES)), _const_spec(((1 + N_FINE) * LBLK, 2 * LBLK)),
                  _const_spec((GRP, GRP)), _const_spec((N_LEVELS - N_FINE, LBLK, LANES)), _const_spec((1, DV)),
                  _const_spec((8, LANES))],
        out_specs=prev_tile,
        out_shape=jax.ShapeDtypeStruct((t, D_MODEL), F32),
        scratch_shapes=[pltpu.VMEM((2, TBLK, VW), BF16), pltpu.VMEM((2, TBLK, VW), BF16),
                        pltpu.VMEM((HEADS, DK, DV), F32), pltpu.VMEM((HEADS, DV, DK), F32)],
        compiler_params=pltpu.CompilerParams(
            dimension_semantics=("arbitrary",), vmem_limit_bytes=MIX_VMEM_LIMIT),
        name="mix",
    )(qa, ka, va, qb, kb, vb, la, h1, nw, wra, wrb, wga, wgb, wa, wb, wo,
      dmask, qdec, kdec, cum, lvl, sgn, gla_nw, jnp.zeros((8, LANES), jnp.int32))


def _rope_tables(seq):
    half = DK // 2
    inv_freq = ROPE_BASE ** (-np.arange(half, dtype=np.float64) / half)
    ang = np.arange(seq, dtype=np.float64)[:, None] * inv_freq[None, :]
    cos = np.concatenate([np.cos(ang), np.cos(ang)], axis=1)
    sin = np.concatenate([-np.sin(ang), np.sin(ang)], axis=1)
    return jnp.asarray(cos, dtype=F32), jnp.asarray(sin, dtype=F32)


def kernel(x, norm_ffn1, ffn1_w1, ffn1_w3, ffn1_w2, norm_mix, w_in, gla_gate_w2, gla_gate_b, gla_norm_w,
           w_branch_ret, w_branch_gla, w_out, norm_ffn2, ffn2_w1, ffn2_w3, ffn2_w2, norm_final):
    batch, seq, _ = x.shape
    depth = norm_ffn1.shape[0]
    t = batch * seq
    h = x.reshape(t, D_MODEL)
    cos_t, sin_t = _rope_tables(seq)
    row = lambda a: a.reshape(1, -1).astype(F32)
    for l in range(depth):
        h = _ffn(h, row(norm_ffn1[l]), ffn1_w1[l].astype(BF16), ffn1_w3[l].astype(BF16),
                 ffn1_w2[l].astype(BF16))

        cols, off = [], 0
        for width in IN_SPLITS:
            cols.append(w_in[l][:, off:off + width])
            off += width
        wqa, wka, wva, wra, wqb, wkb, wvb, wrb, wlr, wga, wgb = cols
        wlr_p = jnp.pad(wlr, ((0, 0), (0, LANES - GATE_RANK))).astype(BF16)
        w2 = jnp.pad(gla_gate_w2[l], ((0, LANES - GATE_RANK), (0, 0)))
        w2h = w2.astype(BF16)
        w2l = (w2 - w2h.astype(F32)).astype(BF16)
        bf = lambda a: a.astype(BF16)
        qa, ka, va, qb, kb, vb, la = _proj(
            h, row(norm_mix[l]), bf(wqa), bf(wka), bf(wva), bf(wqb), bf(wkb), bf(wvb), wlr_p, w2h, w2l,
            row(gla_gate_b[l]), cos_t, sin_t, seq)
        h = _mix(qa, ka, va, qb, kb, vb, la, h, row(norm_mix[l]), bf(wra), bf(wrb), bf(wga), bf(wgb),
                 bf(w_branch_ret[l]), bf(w_branch_gla[l]), bf(w_out[l]), row(gla_norm_w[l]), seq)
        last = l == depth - 1
        h = _ffn(h, row(norm_ffn2[l]), bf(ffn2_w1[l]), bf(ffn2_w3[l]), bf(ffn2_w2[l]),
                 final_w=row(norm_final) if last else None)
    return h.reshape(batch, seq, D_MODEL)
```

```python
import functools
import math

import numpy as np
import jax
import jax.numpy as jnp
from jax import lax
from jax.experimental import pallas as pl
from jax.experimental.pallas import tpu as pltpu

D_MODEL = 1024
CHUNK = 64
HEADS = 4
DK = 128
DV = 256
QK = HEADS * DK
VW = HEADS * DV
GATE_RANK = 16
GATE_TAU = 16.0
D_FF = 2816
ROPE_BASE = 10000.0
NORM_EPS = 1e-6
IN_SPLITS = (QK, QK, VW, VW, QK, QK, VW, VW, GATE_RANK, D_MODEL, D_MODEL)

LANES = 128
TM = 512
TM_FINAL = 1024
LBLK = 256
GRP = LBLK // 2
N_LEVELS = LBLK.bit_length() - 1
N_FINE = 2
TBLK = 512
FF_CHUNKS = ((0, 1024), (1024, 2048), (2048, D_FF))
VMEM_LIMIT = 52 * 1024 * 1024
MIX_VMEM_LIMIT = 60 * 1024 * 1024

F32 = jnp.float32
BF16 = jnp.bfloat16


def _const_spec(shape):
    zeros = (0,) * len(shape)
    return pl.BlockSpec(shape, lambda *_: zeros, pipeline_mode=pl.Buffered(1))


def _rms(x, w):
    ms = jnp.mean(x * x, axis=-1, keepdims=True)
    return x * lax.rsqrt(ms + NORM_EPS) * w


def _dot(a, b):
    return jnp.dot(a, b, preferred_element_type=F32)


def _dot_nt(a, b):
    return lax.dot_general(a, b, (((1,), (1,)), ((), ())), preferred_element_type=F32)


def _dot_tn(a, b):
    return lax.dot_general(a, b, (((0,), (0,)), ((), ())), preferred_element_type=F32)


def _sigmoid(x):
    return 1.0 / (1.0 + jnp.exp(-x))


SLAB = TM // 2


def _swiglu_half_step(x, nw_ref, w1_ref, w3_ref, w2_ref):
    xn = _rms(x, nw_ref[...]).astype(BF16)
    acc = None
    for a, b in FF_CHUNKS:
        a1 = _dot(xn, w1_ref[:, a:b])
        a3 = _dot(xn, w3_ref[:, a:b])
        g = (a1 * _sigmoid(a1) * a3).astype(BF16)
        p = _dot(g, w2_ref[a:b, :])
        acc = p if acc is None else acc + p
    return x + 0.5 * acc


def _proj_slab(h, rows, nw_ref, wqa, wka, wva, wqb, wkb, wvb, wlr, w2p, gb_ref, cos_ref, sin_ref,
               qa_o, ka_o, va_o, qb_o, kb_o, vb_o, la_o):
    scale = DK ** -0.5
    u = _rms(h, nw_ref[...]).astype(BF16)
    cos = cos_ref[rows, :]
    sin = sin_ref[rows, :]

    def rotary(x, o_ref, mul):
        for hd in range(HEADS):
            xh = x[:, hd * DK:(hd + 1) * DK]
            r = xh * cos + pltpu.roll(xh, DK // 2, axis=1) * sin
            if mul != 1.0:
                r = r * mul
            o_ref[rows, hd * DK:(hd + 1) * DK] = r.astype(BF16)

    lr = _dot(u, wlr[...])
    lr_hi = lr.astype(BF16).astype(F32)
    lr_lo = (lr - lr_hi).astype(BF16).astype(F32)
    packed = lr_hi + pltpu.roll(lr_lo, GATE_RANK, axis=1) + pltpu.roll(lr_hi, 2 * GATE_RANK, axis=1)
    z = _dot(packed.astype(BF16), w2p[...]) + gb_ref[...]
    rotary(_dot(u, wqa[...]), qa_o, scale)
    rotary(_dot(u, wka[...]), ka_o, 1.0)
    log_sig = jnp.minimum(z, 0.0) - jnp.log(1.0 + jnp.exp(-jnp.abs(z)))
    la_o[rows, :] = log_sig * (1.0 / GATE_TAU)
    va_o[rows, :] = _dot(u, wva[...]).astype(BF16)
    qb_o[rows, :] = (_dot(u, wqb[...]) * scale).astype(BF16)
    kb_o[rows, :] = _dot(u, wkb[...]).astype(BF16)
    vb_o[rows, :] = _dot(u, wvb[...]).astype(BF16)


def _ffn_kernel(x_ref, nw_ref, w1_ref, w3_ref, w2_ref, fw_ref, o_ref):
    for r0 in range(0, TM_FINAL, SLAB):
        rows = slice(r0, r0 + SLAB)
        h = _swiglu_half_step(x_ref[rows, :], nw_ref, w1_ref, w3_ref, w2_ref)
        o_ref[rows, :] = _rms(h, fw_ref[...])


def _ffn_proj_kernel(x_ref, nw_ref, w1_ref, w3_ref, w2_ref, nwm_ref, *rest):
    proj_in, (h_o, *proj_out) = rest[:11], rest[11:]
    for r0 in range(0, TM, SLAB):
        rows = slice(r0, r0 + SLAB)
        h = _swiglu_half_step(x_ref[rows, :], nw_ref, w1_ref, w3_ref, w2_ref)
        h_o[rows, :] = h
        _proj_slab(h, rows, nwm_ref, *proj_in, *proj_out)


def _ffn_specs(tm):
    tile = pl.BlockSpec((tm, D_MODEL), lambda i: (i, 0))
    return tile, [tile, _const_spec((1, D_MODEL)), _const_spec((D_MODEL, D_FF)),
                  _const_spec((D_MODEL, D_FF)), _const_spec((D_FF, D_MODEL)), _const_spec((1, D_MODEL))]


def _ffn_final(x, nw, w1, w3, w2, final_w):
    t = x.shape[0]
    tile, in_specs = _ffn_specs(TM_FINAL)
    return pl.pallas_call(
        _ffn_kernel,
        grid=(t // TM_FINAL,),
        in_specs=in_specs,
        out_specs=tile,
        out_shape=jax.ShapeDtypeStruct((t, D_MODEL), F32),
        compiler_params=pltpu.CompilerParams(
            dimension_semantics=("parallel",), vmem_limit_bytes=VMEM_LIMIT),
        name="ffn_final",
    )(x, nw, w1, w3, w2, final_w)


def _ffn_proj(x, nw, w1, w3, w2, nw_mix, proj_w, gate_b, cos_t, sin_t, seq):
    t = x.shape[0]
    tiles_per_seq = seq // TM
    tile, in_specs = _ffn_specs(TM)

    def cols(w):
        return pl.BlockSpec((TM, w), lambda i: (i, 0))

    pos_tile = pl.BlockSpec((TM, DK), lambda i: (i % tiles_per_seq, 0))
    in_specs += [_const_spec((D_MODEL, QK)), _const_spec((D_MODEL, QK)), _const_spec((D_MODEL, VW)),
                 _const_spec((D_MODEL, QK)), _const_spec((D_MODEL, QK)), _const_spec((D_MODEL, VW)),
                 _const_spec((D_MODEL, LANES)), _const_spec((LANES, QK)), _const_spec((1, QK)),
                 pos_tile, pos_tile]
    out_specs = [tile, cols(QK), cols(QK), cols(VW), cols(QK), cols(QK), cols(VW), cols(QK)]
    out_shape = [jax.ShapeDtypeStruct((t, D_MODEL), F32)] + [
        jax.ShapeDtypeStruct((t, w), d) for w, d in
        ((QK, BF16), (QK, BF16), (VW, BF16), (QK, BF16), (QK, BF16), (VW, BF16), (QK, F32))]
    return pl.pallas_call(
        _ffn_proj_kernel,
        grid=(t // TM,),
        in_specs=in_specs,
        out_specs=out_specs,
        out_shape=out_shape,
        compiler_params=pltpu.CompilerParams(
            dimension_semantics=("parallel",), vmem_limit_bytes=VMEM_LIMIT),
        name="ffn_proj",
    )(x, nw, w1, w3, w2, nw_mix, *proj_w, gate_b, cos_t, sin_t)


def _ret_tables():
    idx = np.arange(LBLK, dtype=np.float64)
    chunk = np.arange(LBLK) // CHUNK
    diff = idx[:, None] - idx[None, :]
    same = chunk[:, None] == chunk[None, :]
    earlier = chunk[None, :] < chunk[:, None]
    dmask, qdec, kdec, cdec = [], [], [], []
    for h in range(HEADS):
        lg = math.log1p(-2.0 ** (-5.0 - h))
        dmask.append(np.where(same, np.exp(lg * np.abs(diff)),
                              np.where(earlier, np.exp(lg * diff), 0.0)))
        qdec.append(np.broadcast_to(np.exp(lg * (idx + 1.0))[:, None], (LBLK, LANES)))
        kdec.append(np.broadcast_to(np.exp(lg * (LBLK - 1.0 - idx))[:, None], (LBLK, LANES)))
        cdec.append(math.exp(lg * LBLK))
    f = lambda a: jnp.asarray(np.stack(a), dtype=F32)
    return f(dmask), f(qdec), f(kdec), tuple(cdec)


def _gla_tables():
    n = np.arange(LBLK)
    mats = [n[None, :] <= n[:, None]]
    for p in range(N_FINE):
        ref = (n >> (p + 1) << (p + 1)) + (1 << p) - 1
        lo, hi = np.minimum(n, ref), np.maximum(n, ref)
        mats.append((n[None, :] > lo[:, None]) & (n[None, :] <= hi[:, None]))
    cum = np.concatenate(mats, axis=0).astype(np.float32)
    cum = np.concatenate([cum, cum], axis=1)
    m = np.arange(GRP)
    x = m[:, None] ^ m[None, :]
    lvl = np.where(x == 0, -1, np.floor(np.log2(np.maximum(x, 1))).astype(np.int64))
    hidden = (m[None, :] // CHUNK) > (m[:, None] // CHUNK)
    lvl = np.where(hidden, -2, lvl)
    second = ((n[None, :] >> np.arange(N_FINE, N_LEVELS)[:, None]) & 1).astype(np.float64)
    sgn = (2.0 * second - 1.0) * math.log2(math.e)
    sgn = np.broadcast_to(sgn[:, :, None], (N_LEVELS - N_FINE, LBLK, LANES))
    return (jnp.asarray(cum, dtype=BF16), jnp.asarray(lvl, dtype=jnp.int32),
            jnp.asarray(sgn, dtype=F32))


def _split_ref_rows(g, half):
    n = g.shape[0]
    size = 2 * half
    g3 = g.reshape(n // size, size, g.shape[1])
    ref = jnp.broadcast_to(g3[:, half - 1:half, :], g3.shape)
    return ref.reshape(g.shape)


RET_HEAD_COST = 150
GLA_PREP_COST = 250
GLA_SETUP_COST = 60
GLA_LEVEL_COST = 100
GLA_FINISH_COST = 250
MIXER_COST = (TBLK // LBLK) * (HEADS * RET_HEAD_COST + GLA_PREP_COST
                               + HEADS * (GLA_SETUP_COST + N_LEVELS * GLA_LEVEL_COST + GLA_FINISH_COST))
MERGE_NORM_COST = 300
MERGE_ROWS = 256
MERGE_PIECE_COLS = 512
MERGE_PIECE_COST = MERGE_ROWS * (D_MODEL // 256) * (MERGE_PIECE_COLS // 256) // 4
MERGE_COST = (TBLK // MERGE_ROWS) * (MERGE_NORM_COST
                                     + 7 * (D_MODEL // MERGE_PIECE_COLS) * MERGE_PIECE_COST)


def _interleave(gen_a, total_a, gen_b, total_b):
    gens, totals, done, at = [gen_a, gen_b], [total_a, total_b], [False, False], [0.0, 0.0]
    while not all(done):
        i = 0 if (not done[0] and (done[1] or at[0] <= at[1])) else 1
        try:
            at[i] += next(gens[i]) / totals[i]
        except StopIteration:
            done[i] = True


def _ret_block(q_ref, k_ref, v_ref, rows, dmask_ref, qdec_ref, kdec_ref, cdec, state_ref, o_ref, slot):
    for h in range(HEADS):
        qk_cols = slice(h * DK, (h + 1) * DK)
        v_cols = slice(h * DV, (h + 1) * DV)
        q = q_ref[rows, qk_cols]
        k = k_ref[rows, qk_cols]
        v = v_ref[rows, v_cols]
        st = state_ref[h]
        s = _dot_nt(q, k) * dmask_ref[h]
        qdec = qdec_ref[h]
        inter = _dot(q, st.astype(BF16)) * jnp.concatenate([qdec, qdec], axis=1)
        o = _dot(s.astype(BF16), v) + inter
        kd = (k.astype(F32) * kdec_ref[h]).astype(BF16)
        state_ref[h] = cdec[h] * st + _dot_tn(kd, v)
        mu = jnp.mean(o, axis=-1, keepdims=True)
        oc = o - mu
        var = jnp.mean(oc * oc, axis=-1, keepdims=True)
        o_ref[slot, rows, v_cols] = (oc * lax.rsqrt(var + NORM_EPS)).astype(BF16)
        yield RET_HEAD_COST


def _gla_block(q_ref, k_ref, v_ref, la_ref, rows, cum_ref, lvl_ref, sgn_ref, nw_ref, state_ref,
               o_ref, slot, pace):
    top, bot = slice(0, GRP), slice(GRP, LBLK)
    la = la_ref[rows, :]
    la_hi = la.astype(BF16)
    la_lo = (la - la_hi.astype(F32)).astype(BF16)
    cum = _dot(cum_ref[...], jnp.concatenate([la_hi, la_lo], axis=0))
    g_all = cum[0:LBLK]
    lvl = lvl_ref[...]
    yield GLA_PREP_COST
    for h in range(HEADS):
        qk_cols = slice(h * DK, (h + 1) * DK)
        v_cols = slice(h * DV, (h + 1) * DV)
        g = pace(g_all[:, qk_cols])
        qb = q_ref[rows, qk_cols]
        kb = k_ref[rows, qk_cols]
        v = v_ref[rows, v_cols]
        q = qb.astype(F32)
        k = kb.astype(F32)
        s_grp = [jnp.where(lvl == -1, _dot_nt(qb[r], kb[r]), 0.0) for r in (top, bot)]
        s_cross = None
        yield GLA_SETUP_COST
        for p in range(N_LEVELS):
            if p < N_FINE:
                e = jnp.exp(cum[(p + 1) * LBLK:(p + 2) * LBLK, qk_cols])
            else:
                e = jnp.exp2((g - _split_ref_rows(g, 1 << p)) * sgn_ref[p - N_FINE])
            qe = (q * e).astype(BF16)
            ke = (k * e).astype(BF16)
            if (2 << p) <= GRP:
                s_grp = [jnp.where(lvl == p, _dot_nt(qe[r], ke[r]), s)
                         for r, s in zip((top, bot), s_grp)]
            else:
                s_cross = _dot_nt(qe[bot], ke[top])
            yield GLA_LEVEL_COST
        st = state_ref[h]
        inter = _dot_nt((q * jnp.exp(g)).astype(BF16), st.astype(BF16))
        o_top = _dot(s_grp[0].astype(BF16), v[top])
        o_bot = _dot(jnp.concatenate([s_cross, s_grp[1]], axis=1).astype(BF16), v)
        o = jnp.concatenate([o_top, o_bot], axis=0) + inter
        g_last = g[LBLK - 1:LBLK, :]
        kd = (k * jnp.exp(g_last - g)).astype(BF16)
        state_ref[h] = st * jnp.exp(g_last) + _dot_tn(v, kd)
        ms = jnp.mean(o * o, axis=-1, keepdims=True)
        o_ref[slot, rows, v_cols] = (o * lax.rsqrt(ms + NORM_EPS) * nw_ref[...]).astype(BF16)
        yield GLA_FINISH_COST


def _mix_kernel(qa_ref, ka_ref, va_ref, qb_ref, kb_ref, vb_ref, la_ref, h_ref, nw_ref,
                wra, wrb, wga, wgb, wa, wb, wo,
                dmask_ref, qdec_ref, kdec_ref, cum_ref, lvl_ref, sgn_ref, gnw_ref, zero_ref,
                o_ref, oa_scr, ob_scr, ret_state, gla_state, *, cdec, blocks_per_seq, n_blocks):
    s = pl.program_id(0)
    blk = jnp.minimum(s, n_blocks - 1)
    slot = lax.rem(s, 2)

    @pl.when(s == 0)
    def _():
        oa_scr[...] = jnp.zeros_like(oa_scr)
        ob_scr[...] = jnp.zeros_like(ob_scr)

    @pl.when(lax.rem(blk, blocks_per_seq) == 0)
    def _():
        ret_state[...] = jnp.zeros_like(ret_state)
        gla_state[...] = jnp.zeros_like(gla_state)

    newest = [None]

    def pace(x):
        if newest[0] is None:
            return x
        bits = lax.bitcast_convert_type(newest[0][0:8, 0:LANES], jnp.int32) & zero_ref[...]
        zero = lax.bitcast_convert_type(bits, F32)
        return (x.reshape(-1, 8, LANES) + zero[None]).reshape(x.shape)

    def mixers():
        for i in range(TBLK // LBLK):
            rows = slice(i * LBLK, (i + 1) * LBLK)
            yield from _ret_block(qa_ref, ka_ref, va_ref, rows, dmask_ref, qdec_ref, kdec_ref, cdec,
                                  ret_state, oa_scr, slot)
            yield from _gla_block(qb_ref, kb_ref, vb_ref, la_ref, rows, cum_ref, lvl_ref, sgn_ref,
                                  gnw_ref, gla_state, ob_scr, slot, pace)

    prev = 1 - slot
    pieces = [slice(c, c + MERGE_PIECE_COLS) for c in range(0, D_MODEL, MERGE_PIECE_COLS)]

    def dot_pieces(x, w_ref, out):
        for c in pieces:
            out.append(_dot(x, w_ref[:, c]))
            newest[0] = out[-1]
            yield MERGE_PIECE_COST

    def merge_slab(rows):
        u = _rms(h_ref[rows, :], nw_ref[...]).astype(BF16)
        yield MERGE_NORM_COST
        gated = []
        for o_scr, w_gate_out, w_branch, w_gate_merge in ((oa_scr, wra, wa, wga), (ob_scr, wrb, wb, wgb)):
            r, y, gm = [], [], []
            yield from dot_pieces(u, w_gate_out, r)
            og = jnp.concatenate(
                [(ri * _sigmoid(ri) * o_scr[prev, rows, c].astype(F32)).astype(BF16)
                 for ri, c in zip(r, pieces)], axis=1)
            yield from dot_pieces(og, w_branch, y)
            yield from dot_pieces(u, w_gate_merge, gm)
            gated.append([_sigmoid(gi) * yi for gi, yi in zip(gm, y)])
        merged = jnp.concatenate([(a + b).astype(BF16) for a, b in zip(*gated)], axis=1)
        out = []
        yield from dot_pieces(merged, wo, out)
        for c, oi in zip(pieces, out):
            o_ref[rows, c] = h_ref[rows, c] + oi

    def merge():
        for r0 in range(0, TBLK, MERGE_ROWS):
            yield from merge_slab(slice(r0, r0 + MERGE_ROWS))

    _interleave(mixers(), MIXER_COST, merge(), MERGE_COST)


def _mix(qa, ka, va, qb, kb, vb, la, h1, nw, wra, wrb, wga, wgb, wa, wb, wo, gla_nw, seq):
    t = h1.shape[0]
    n_blocks = t // TBLK
    dmask, qdec, kdec, cdec = _ret_tables()
    cum, lvl, sgn = _gla_tables()

    def cur(w):
        return pl.BlockSpec((TBLK, w), lambda s: (jnp.minimum(s, n_blocks - 1), 0))

    prev_tile = pl.BlockSpec((TBLK, D_MODEL), lambda s: (jnp.maximum(s - 1, 0), 0))
    sq = _const_spec((D_MODEL, D_MODEL))
    return pl.pallas_call(
        functools.partial(_mix_kernel, cdec=cdec, blocks_per_seq=seq // TBLK, n_blocks=n_blocks),
        grid=(n_blocks + 1,),
        in_specs=[cur(QK), cur(QK), cur(VW), cur(QK), cur(QK), cur(VW), cur(QK), prev_tile,
                  _const_spec((1, D_MODEL)), sq, sq, sq, sq, sq, sq, sq,
                  _const_spec((HEADS, LBLK, LBLK)), _const_spec((HEADS, LBLK, LANES)),
                  _const_spec((HEADS, LBLK, LANES)), _const_spec(((1 + N_FINE) * LBLK, 2 * LBLK)),
                  _const_spec((GRP, GRP)), _const_spec((N_LEVELS - N_FINE, LBLK, LANES)), _const_spec((1, DV)),
                  _const_spec((8, LANES))],
        out_specs=prev_tile,
        out_shape=jax.ShapeDtypeStruct((t, D_MODEL), F32),
        scratch_shapes=[pltpu.VMEM((2, TBLK, VW), BF16), pltpu.VMEM((2, TBLK, VW), BF16),
                        pltpu.VMEM((HEADS, DK, DV), F32), pltpu.VMEM((HEADS, DV, DK), F32)],
        compiler_params=pltpu.CompilerParams(
            dimension_semantics=("arbitrary",), vmem_limit_bytes=MIX_VMEM_LIMIT),
        name="mix",
    )(qa, ka, va, qb, kb, vb, la, h1, nw, wra, wrb, wga, wgb, wa, wb, wo,
      dmask, qdec, kdec, cum, lvl, sgn, gla_nw, jnp.zeros((8, LANES), jnp.int32))


def _rope_tables(seq):
    half = DK // 2
    inv_freq = ROPE_BASE ** (-np.arange(half, dtype=np.float64) / half)
    ang = np.arange(seq, dtype=np.float64)[:, None] * inv_freq[None, :]
    cos = np.concatenate([np.cos(ang), np.cos(ang)], axis=1)
    sin = np.concatenate([-np.sin(ang), np.sin(ang)], axis=1)
    return jnp.asarray(cos, dtype=F32), jnp.asarray(sin, dtype=F32)


def kernel(x, norm_ffn1, ffn1_w1, ffn1_w3, ffn1_w2, norm_mix, w_in, gla_gate_w2, gla_gate_b, gla_norm_w,
           w_branch_ret, w_branch_gla, w_out, norm_ffn2, ffn2_w1, ffn2_w3, ffn2_w2, norm_final):
    batch, seq, _ = x.shape
    depth = norm_ffn1.shape[0]
    t = batch * seq
    h = x.reshape(t, D_MODEL)
    cos_t, sin_t = _rope_tables(seq)
    row = lambda a: a.reshape(1, -1).astype(F32)
    bf = lambda a: a.astype(BF16)
    assert depth == 1, "the block's final RMSNorm is fused into the layer's closing FFN call"
    for l in range(depth):
        cols, off = [], 0
        for width in IN_SPLITS:
            cols.append(w_in[l][:, off:off + width])
            off += width
        wqa, wka, wva, wra, wqb, wkb, wvb, wrb, wlr, wga, wgb = cols
        wlr_p = jnp.pad(wlr, ((0, 0), (0, LANES - GATE_RANK))).astype(BF16)
        w2h = gla_gate_w2[l].astype(BF16)
        w2l = (gla_gate_w2[l] - w2h.astype(F32)).astype(BF16)
        w2p = jnp.pad(jnp.concatenate([w2h, w2h, w2l], axis=0), ((0, LANES - 3 * GATE_RANK), (0, 0)))
        proj_w = (bf(wqa), bf(wka), bf(wva), bf(wqb), bf(wkb), bf(wvb), wlr_p, w2p)
        h, qa, ka, va, qb, kb, vb, la = _ffn_proj(
            h, row(norm_ffn1[l]), bf(ffn1_w1[l]), bf(ffn1_w3[l]), bf(ffn1_w2[l]), row(norm_mix[l]),
            proj_w, row(gla_gate_b[l]), cos_t, sin_t, seq)
        h = _mix(qa, ka, va, qb, kb, vb, la, h, row(norm_mix[l]), bf(wra), bf(wrb), bf(wga), bf(wgb),
                 bf(w_branch_ret[l]), bf(w_branch_gla[l]), bf(w_out[l]), row(gla_norm_w[l]), seq)
        h = _ffn_final(h, row(norm_ffn2[l]), bf(ffn2_w1[l]), bf(ffn2_w3[l]), bf(ffn2_w2[l]), row(norm_final))
    return h.reshape(batch, seq, D_MODEL)
```

```python
import functools
import math

import numpy as np
import jax
import jax.numpy as jnp
from jax import lax
from jax.experimental import pallas as pl
from jax.experimental.pallas import tpu as pltpu

D_MODEL = 1024
CHUNK = 64
HEADS = 4
DK = 128
DV = 256
QK = HEADS * DK
VW = HEADS * DV
GATE_RANK = 16
GATE_TAU = 16.0
D_FF = 2816
ROPE_BASE = 10000.0
NORM_EPS = 1e-6
IN_SPLITS = (QK, QK, VW, VW, QK, QK, VW, VW, GATE_RANK, D_MODEL, D_MODEL)

LANES = 128
TM = 512
TM_FINAL = 1024
LBLK = 256
GRP = LBLK // 2
N_LEVELS = LBLK.bit_length() - 1
N_FINE = 2
TBLK = 512
FF_CHUNKS = ((0, 1024), (1024, 2048), (2048, D_FF))
VMEM_LIMIT = 52 * 1024 * 1024
MIX_VMEM_LIMIT = 60 * 1024 * 1024

F32 = jnp.float32
BF16 = jnp.bfloat16


def _const_spec(shape):
    zeros = (0,) * len(shape)
    return pl.BlockSpec(shape, lambda *_: zeros, pipeline_mode=pl.Buffered(1))


def _rms(x, w):
    ms = jnp.mean(x * x, axis=-1, keepdims=True)
    return x * lax.rsqrt(ms + NORM_EPS) * w


def _dot(a, b):
    return jnp.dot(a, b, preferred_element_type=F32)


def _dot_nt(a, b):
    return lax.dot_general(a, b, (((1,), (1,)), ((), ())), preferred_element_type=F32)


def _dot_tn(a, b):
    return lax.dot_general(a, b, (((0,), (0,)), ((), ())), preferred_element_type=F32)


def _sigmoid(x):
    return 1.0 / (1.0 + jnp.exp(-x))


SLAB = TM // 2


def _swiglu_half_step(x, nw_ref, w1_ref, w3_ref, w2_ref):
    xn = _rms(x, nw_ref[...]).astype(BF16)
    acc = None
    for a, b in FF_CHUNKS:
        a1 = _dot(xn, w1_ref[:, a:b])
        a3 = _dot(xn, w3_ref[:, a:b])
        g = (a1 * _sigmoid(a1) * a3).astype(BF16)
        p = _dot(g, w2_ref[a:b, :])
        acc = p if acc is None else acc + p
    return x + 0.5 * acc


def _proj_slab(h, rows, nw_ref, wqa, wka, wva, wqb, wkb, wvb, wlr, w2p, gb_ref, cos_ref, sin_ref,
               qa_o, ka_o, va_o, qb_o, kb_o, vb_o, la_o):
    scale = DK ** -0.5
    u = _rms(h, nw_ref[...]).astype(BF16)
    cos = cos_ref[rows, :]
    sin = sin_ref[rows, :]

    def rotary(x, o_ref, mul):
        for hd in range(HEADS):
            xh = x[:, hd * DK:(hd + 1) * DK]
            r = xh * cos + pltpu.roll(xh, DK // 2, axis=1) * sin
            if mul != 1.0:
                r = r * mul
            o_ref[rows, hd * DK:(hd + 1) * DK] = r.astype(BF16)

    lr = _dot(u, wlr[...])
    lr_hi = lr.astype(BF16).astype(F32)
    lr_lo = (lr - lr_hi).astype(BF16).astype(F32)
    packed = lr_hi + pltpu.roll(lr_lo, GATE_RANK, axis=1) + pltpu.roll(lr_hi, 2 * GATE_RANK, axis=1)
    z = _dot(packed.astype(BF16), w2p[...]) + gb_ref[...]
    rotary(_dot(u, wqa[...]), qa_o, scale)
    rotary(_dot(u, wka[...]), ka_o, 1.0)
    log_sig = jnp.minimum(z, 0.0) - jnp.log(1.0 + jnp.exp(-jnp.abs(z)))
    la_o[rows, :] = log_sig * (1.0 / GATE_TAU)
    va_o[rows, :] = _dot(u, wva[...]).astype(BF16)
    qb_o[rows, :] = (_dot(u, wqb[...]) * scale).astype(BF16)
    kb_o[rows, :] = _dot(u, wkb[...]).astype(BF16)
    vb_o[rows, :] = _dot(u, wvb[...]).astype(BF16)


def _ffn_kernel(x_ref, nw_ref, w1_ref, w3_ref, w2_ref, fw_ref, o_ref):
    for r0 in range(0, TM_FINAL, SLAB):
        rows = slice(r0, r0 + SLAB)
        h = _swiglu_half_step(x_ref[rows, :], nw_ref, w1_ref, w3_ref, w2_ref)
        o_ref[rows, :] = _rms(h, fw_ref[...])


def _ffn_proj_kernel(x_ref, nw_ref, w1_ref, w3_ref, w2_ref, nwm_ref, *rest):
    proj_in, (h_o, *proj_out) = rest[:11], rest[11:]
    for r0 in range(0, TM, SLAB):
        rows = slice(r0, r0 + SLAB)
        h = _swiglu_half_step(x_ref[rows, :], nw_ref, w1_ref, w3_ref, w2_ref)
        h_o[rows, :] = h
        _proj_slab(h, rows, nwm_ref, *proj_in, *proj_out)


def _ffn_specs(tm):
    tile = pl.BlockSpec((tm, D_MODEL), lambda i: (i, 0))
    return tile, [tile, _const_spec((1, D_MODEL)), _const_spec((D_MODEL, D_FF)),
                  _const_spec((D_MODEL, D_FF)), _const_spec((D_FF, D_MODEL)), _const_spec((1, D_MODEL))]


def _ffn_final(x, nw, w1, w3, w2, final_w):
    t = x.shape[0]
    tile, in_specs = _ffn_specs(TM_FINAL)
    return pl.pallas_call(
        _ffn_kernel,
        grid=(t // TM_FINAL,),
        in_specs=in_specs,
        out_specs=tile,
        out_shape=jax.ShapeDtypeStruct((t, D_MODEL), F32),
        compiler_params=pltpu.CompilerParams(
            dimension_semantics=("parallel",), vmem_limit_bytes=VMEM_LIMIT),
        name="ffn_final",
    )(x, nw, w1, w3, w2, final_w)


def _ffn_proj(x, nw, w1, w3, w2, nw_mix, proj_w, gate_b, cos_t, sin_t, seq):
    t = x.shape[0]
    tiles_per_seq = seq // TM
    tile, in_specs = _ffn_specs(TM)

    def cols(w):
        return pl.BlockSpec((TM, w), lambda i: (i, 0))

    pos_tile = pl.BlockSpec((TM, DK), lambda i: (i % tiles_per_seq, 0))
    in_specs += [_const_spec((D_MODEL, QK)), _const_spec((D_MODEL, QK)), _const_spec((D_MODEL, VW)),
                 _const_spec((D_MODEL, QK)), _const_spec((D_MODEL, QK)), _const_spec((D_MODEL, VW)),
                 _const_spec((D_MODEL, LANES)), _const_spec((LANES, QK)), _const_spec((1, QK)),
                 pos_tile, pos_tile]
    out_specs = [tile, cols(QK), cols(QK), cols(VW), cols(QK), cols(QK), cols(VW), cols(QK)]
    out_shape = [jax.ShapeDtypeStruct((t, D_MODEL), F32)] + [
        jax.ShapeDtypeStruct((t, w), d) for w, d in
        ((QK, BF16), (QK, BF16), (VW, BF16), (QK, BF16), (QK, BF16), (VW, BF16), (QK, F32))]
    return pl.pallas_call(
        _ffn_proj_kernel,
        grid=(t // TM,),
        in_specs=in_specs,
        out_specs=out_specs,
        out_shape=out_shape,
        compiler_params=pltpu.CompilerParams(
            dimension_semantics=("parallel",), vmem_limit_bytes=VMEM_LIMIT),
        name="ffn_proj",
    )(x, nw, w1, w3, w2, nw_mix, *proj_w, gate_b, cos_t, sin_t)


def _ret_tables():
    idx = np.arange(LBLK, dtype=np.float64)
    chunk = np.arange(LBLK) // CHUNK
    diff = idx[:, None] - idx[None, :]
    same = chunk[:, None] == chunk[None, :]
    earlier = chunk[None, :] < chunk[:, None]
    dmask, qdec, kdec, cdec = [], [], [], []
    for h in range(HEADS):
        lg = math.log1p(-2.0 ** (-5.0 - h))
        dmask.append(np.where(same, np.exp(lg * np.abs(diff)),
                              np.where(earlier, np.exp(lg * diff), 0.0)))
        qdec.append(np.broadcast_to(np.exp(lg * (idx + 1.0))[:, None], (LBLK, LANES)))
        kdec.append(np.broadcast_to(np.exp(lg * (LBLK - 1.0 - idx))[:, None], (LBLK, LANES)))
        cdec.append(math.exp(lg * LBLK))
    f = lambda a: jnp.asarray(np.stack(a), dtype=F32)
    return f(dmask), f(qdec), f(kdec), tuple(cdec)


def _gla_tables():
    n = np.arange(LBLK)
    mats = [n[None, :] <= n[:, None]]
    for p in range(N_FINE):
        ref = (n >> (p + 1) << (p + 1)) + (1 << p) - 1
        lo, hi = np.minimum(n, ref), np.maximum(n, ref)
        mats.append((n[None, :] > lo[:, None]) & (n[None, :] <= hi[:, None]))
    cum = np.concatenate(mats, axis=0).astype(np.float32)
    cum = np.concatenate([cum, cum], axis=1)
    m = np.arange(GRP)
    x = m[:, None] ^ m[None, :]
    lvl = np.where(x == 0, -1, np.floor(np.log2(np.maximum(x, 1))).astype(np.int64))
    hidden = (m[None, :] // CHUNK) > (m[:, None] // CHUNK)
    lvl = np.where(hidden, -2, lvl)
    second = ((n[None, :] >> np.arange(N_FINE, N_LEVELS)[:, None]) & 1).astype(np.float64)
    sgn = (2.0 * second - 1.0) * math.log2(math.e)
    sgn = np.broadcast_to(sgn[:, :, None], (N_LEVELS - N_FINE, LBLK, LANES))
    tri = np.where(m[None, :] <= m[:, None], 1, np.where(hidden, 0, 2))
    return (jnp.asarray(cum, dtype=BF16), jnp.asarray(lvl, dtype=jnp.int32),
            jnp.asarray(sgn, dtype=F32), jnp.asarray(tri, dtype=jnp.int32))


def _split_ref_rows(g, half):
    n = g.shape[0]
    size = 2 * half
    g3 = g.reshape(n // size, size, g.shape[1])
    ref = jnp.broadcast_to(g3[:, half - 1:half, :], g3.shape)
    return ref.reshape(g.shape)


RET_HEAD_COST = 150
RETENTION_COST = (TBLK // LBLK) * HEADS * RET_HEAD_COST
GLA_MILD_DECAY = 60.0
MERGE_NORM_COST = 300
MERGE_ROWS = 256
MERGE_PIECE_COLS = 512
MERGE_PIECE_COST = MERGE_ROWS * (D_MODEL // 256) * (MERGE_PIECE_COLS // 256) // 4
MERGE_COST = (TBLK // MERGE_ROWS) * (MERGE_NORM_COST
                                     + 7 * (D_MODEL // MERGE_PIECE_COLS) * MERGE_PIECE_COST)


def _interleave(gen_a, total_a, gen_b, total_b):
    gens, totals, done, at = [gen_a, gen_b], [total_a, total_b], [False, False], [0.0, 0.0]
    while not all(done):
        i = 0 if (not done[0] and (done[1] or at[0] <= at[1])) else 1
        try:
            at[i] += next(gens[i]) / totals[i]
        except StopIteration:
            done[i] = True


def _ret_block(q_ref, k_ref, v_ref, rows, dmask_ref, qdec_ref, kdec_ref, cdec, state_ref, o_ref, slot):
    for h in range(HEADS):
        qk_cols = slice(h * DK, (h + 1) * DK)
        v_cols = slice(h * DV, (h + 1) * DV)
        q = q_ref[rows, qk_cols]
        k = k_ref[rows, qk_cols]
        v = v_ref[rows, v_cols]
        st = state_ref[h]
        s = _dot_nt(q, k) * dmask_ref[h]
        qdec = qdec_ref[h]
        inter = _dot(q, st.astype(BF16)) * jnp.concatenate([qdec, qdec], axis=1)
        o = _dot(s.astype(BF16), v) + inter
        kd = (k.astype(F32) * kdec_ref[h]).astype(BF16)
        state_ref[h] = cdec[h] * st + _dot_tn(kd, v)
        mu = jnp.mean(o, axis=-1, keepdims=True)
        oc = o - mu
        var = jnp.mean(oc * oc, axis=-1, keepdims=True)
        o_ref[slot, rows, v_cols] = (oc * lax.rsqrt(var + NORM_EPS)).astype(BF16)
        yield RET_HEAD_COST


def _gla_cum(la_ref, rows, cum_ref):
    la = la_ref[rows, :]
    la_hi = la.astype(BF16)
    la_lo = (la - la_hi.astype(F32)).astype(BF16)
    return _dot(cum_ref[...], jnp.concatenate([la_hi, la_lo], axis=0))


def _gla_finish(h, q, k, v, g, s_top, s_bot_row, state_ref, nw_ref, o_ref, slot, rows):
    top = slice(0, GRP)
    v_cols = slice(h * DV, (h + 1) * DV)
    st = state_ref[h]
    inter = _dot_nt((q * jnp.exp(g)).astype(BF16), st.astype(BF16))
    o_top = _dot(s_top.astype(BF16), v[top])
    o_bot = _dot(s_bot_row.astype(BF16), v)
    o = jnp.concatenate([o_top, o_bot], axis=0) + inter
    g_last = g[LBLK - 1:LBLK, :]
    kd = (k * jnp.exp(g_last - g)).astype(BF16)
    state_ref[h] = st * jnp.exp(g_last) + _dot_tn(v, kd)
    ms = jnp.mean(o * o, axis=-1, keepdims=True)
    o_ref[slot, rows, v_cols] = (o * lax.rsqrt(ms + NORM_EPS) * nw_ref[...]).astype(BF16)


def _gla_block(q_ref, k_ref, v_ref, rows, cum, lvl_ref, sgn_ref, nw_ref, state_ref, o_ref, slot):
    top, bot = slice(0, GRP), slice(GRP, LBLK)
    lvl = lvl_ref[...]
    for h in range(HEADS):
        qk_cols = slice(h * DK, (h + 1) * DK)
        g = cum[0:LBLK, qk_cols]
        qb = q_ref[rows, qk_cols]
        kb = k_ref[rows, qk_cols]
        v = v_ref[rows, h * DV:(h + 1) * DV]
        q = qb.astype(F32)
        k = kb.astype(F32)
        s_grp = [jnp.where(lvl == -1, _dot_nt(qb[r], kb[r]), 0.0) for r in (top, bot)]
        s_cross = None
        for p in range(N_LEVELS):
            if p < N_FINE:
                e = jnp.exp(cum[(p + 1) * LBLK:(p + 2) * LBLK, qk_cols])
            else:
                e = jnp.exp2((g - _split_ref_rows(g, 1 << p)) * sgn_ref[p - N_FINE])
            qe = (q * e).astype(BF16)
            ke = (k * e).astype(BF16)
            if (2 << p) <= GRP:
                s_grp = [jnp.where(lvl == p, _dot_nt(qe[r], ke[r]), s)
                         for r, s in zip((top, bot), s_grp)]
            else:
                s_cross = _dot_nt(qe[bot], ke[top])
        _gla_finish(h, q, k, v, g, s_grp[0], jnp.concatenate([s_cross, s_grp[1]], axis=1),
                    state_ref, nw_ref, o_ref, slot, rows)


def _gla_block_mild(q_ref, k_ref, v_ref, rows, cum, tri_ref, nw_ref, state_ref, o_ref, slot):
    top, bot = slice(0, GRP), slice(GRP, LBLK)
    tri = tri_ref[...]
    for h in range(HEADS):
        qk_cols = slice(h * DK, (h + 1) * DK)
        g = cum[0:LBLK, qk_cols]
        q = q_ref[rows, qk_cols].astype(F32)
        k = k_ref[rows, qk_cols].astype(F32)
        v = v_ref[rows, h * DV:(h + 1) * DV]
        g0 = g[0:1, :]
        shrink = jnp.exp(g - g0)
        grow = jnp.exp(g0 - g)
        q_s, q_g = (q * shrink).astype(BF16), (q * grow).astype(BF16)
        k_s, k_g = (k * shrink).astype(BF16), (k * grow).astype(BF16)

        def group_scores(r):
            after = _dot_nt(q_s[r], k_g[r])
            before = _dot_nt(q_g[r], k_s[r])
            return jnp.where(tri == 1, after, jnp.where(tri == 2, before, 0.0))

        s_bot_row = jnp.concatenate([_dot_nt(q_s[bot], k_g[top]), group_scores(bot)], axis=1)
        _gla_finish(h, q, k, v, g, group_scores(top), s_bot_row, state_ref, nw_ref, o_ref, slot, rows)


def _mix_kernel(qa_ref, ka_ref, va_ref, qb_ref, kb_ref, vb_ref, la_ref, h_ref, nw_ref,
                wra, wrb, wga, wgb, wa, wb, wo,
                dmask_ref, qdec_ref, kdec_ref, cum_ref, lvl_ref, sgn_ref, tri_ref, gnw_ref,
                o_ref, oa_scr, ob_scr, ret_state, gla_state, *, cdec, blocks_per_seq, n_blocks):
    s = pl.program_id(0)
    blk = jnp.minimum(s, n_blocks - 1)
    slot = lax.rem(s, 2)
    prev = 1 - slot
    blocks = [slice(i * LBLK, (i + 1) * LBLK) for i in range(TBLK // LBLK)]

    @pl.when(s == 0)
    def _():
        oa_scr[...] = jnp.zeros_like(oa_scr)
        ob_scr[...] = jnp.zeros_like(ob_scr)

    @pl.when(lax.rem(blk, blocks_per_seq) == 0)
    def _():
        ret_state[...] = jnp.zeros_like(ret_state)
        gla_state[...] = jnp.zeros_like(gla_state)

    cums = [_gla_cum(la_ref, rows, cum_ref) for rows in blocks]
    decay = [jnp.max(c[0:1, :] - c[LBLK - 1:LBLK, :]) for c in cums]
    mild = functools.reduce(jnp.maximum, decay) < GLA_MILD_DECAY

    @pl.when(mild)
    def _():
        for rows, c in zip(blocks, cums):
            _gla_block_mild(qb_ref, kb_ref, vb_ref, rows, c, tri_ref, gnw_ref, gla_state, ob_scr, slot)

    @pl.when(jnp.logical_not(mild))
    def _():
        for rows, c in zip(blocks, cums):
            _gla_block(qb_ref, kb_ref, vb_ref, rows, c, lvl_ref, sgn_ref, gnw_ref, gla_state, ob_scr, slot)

    def retention():
        for rows in blocks:
            yield from _ret_block(qa_ref, ka_ref, va_ref, rows, dmask_ref, qdec_ref, kdec_ref, cdec,
                                  ret_state, oa_scr, slot)

    pieces = [slice(c, c + MERGE_PIECE_COLS) for c in range(0, D_MODEL, MERGE_PIECE_COLS)]

    def dot_pieces(x, w_ref, out):
        for c in pieces:
            out.append(_dot(x, w_ref[:, c]))
            yield MERGE_PIECE_COST

    def merge_slab(rows):
        u = _rms(h_ref[rows, :], nw_ref[...]).astype(BF16)
        yield MERGE_NORM_COST
        gated = []
        for o_scr, w_gate_out, w_branch, w_gate_merge in ((oa_scr, wra, wa, wga), (ob_scr, wrb, wb, wgb)):
            r, y, gm = [], [], []
            yield from dot_pieces(u, w_gate_out, r)
            og = jnp.concatenate(
                [(ri * _sigmoid(ri) * o_scr[prev, rows, c].astype(F32)).astype(BF16)
                 for ri, c in zip(r, pieces)], axis=1)
            yield from dot_pieces(og, w_branch, y)
            yield from dot_pieces(u, w_gate_merge, gm)
            gated.append([_sigmoid(gi) * yi for gi, yi in zip(gm, y)])
        merged = jnp.concatenate([(a + b).astype(BF16) for a, b in zip(*gated)], axis=1)
        out = []
        yield from dot_pieces(merged, wo, out)
        for c, oi in zip(pieces, out):
            o_ref[rows, c] = h_ref[rows, c] + oi

    def merge():
        for r0 in range(0, TBLK, MERGE_ROWS):
            yield from merge_slab(slice(r0, r0 + MERGE_ROWS))

    _interleave(retention(), RETENTION_COST, merge(), MERGE_COST)


def _mix(qa, ka, va, qb, kb, vb, la, h1, nw, wra, wrb, wga, wgb, wa, wb, wo, gla_nw, seq):
    t = h1.shape[0]
    n_blocks = t // TBLK
    dmask, qdec, kdec, cdec = _ret_tables()
    cum, lvl, sgn, tri = _gla_tables()

    def cur(w):
        return pl.BlockSpec((TBLK, w), lambda s: (jnp.minimum(s, n_blocks - 1), 0))

    prev_tile = pl.BlockSpec((TBLK, D_MODEL), lambda s: (jnp.maximum(s - 1, 0), 0))
    sq = _const_spec((D_MODEL, D_MODEL))
    return pl.pallas_call(
        functools.partial(_mix_kernel, cdec=cdec, blocks_per_seq=seq // TBLK, n_blocks=n_blocks),
        grid=(n_blocks + 1,),
        in_specs=[cur(QK), cur(QK), cur(VW), cur(QK), cur(QK), cur(VW), cur(QK), prev_tile,
                  _const_spec((1, D_MODEL)), sq, sq, sq, sq, sq, sq, sq,
                  _const_spec((HEADS, LBLK, LBLK)), _const_spec((HEADS, LBLK, LANES)),
                  _const_spec((HEADS, LBLK, LANES)), _const_spec(((1 + N_FINE) * LBLK, 2 * LBLK)),
                  _const_spec((GRP, GRP)), _const_spec((N_LEVELS - N_FINE, LBLK, LANES)),
                  _const_spec((GRP, GRP)), _const_spec((1, DV))],
        out_specs=prev_tile,
        out_shape=jax.ShapeDtypeStruct((t, D_MODEL), F32),
        scratch_shapes=[pltpu.VMEM((2, TBLK, VW), BF16), pltpu.VMEM((2, TBLK, VW), BF16),
                        pltpu.VMEM((HEADS, DK, DV), F32), pltpu.VMEM((HEADS, DV, DK), F32)],
        compiler_params=pltpu.CompilerParams(
            dimension_semantics=("arbitrary",), vmem_limit_bytes=MIX_VMEM_LIMIT),
        name="mix",
    )(qa, ka, va, qb, kb, vb, la, h1, nw, wra, wrb, wga, wgb, wa, wb, wo,
      dmask, qdec, kdec, cum, lvl, sgn, tri, gla_nw)


def _rope_tables(seq):
    half = DK // 2
    inv_freq = ROPE_BASE ** (-np.arange(half, dtype=np.float64) / half)
    ang = np.arange(seq, dtype=np.float64)[:, None] * inv_freq[None, :]
    cos = np.concatenate([np.cos(ang), np.cos(ang)], axis=1)
    sin = np.concatenate([-np.sin(ang), np.sin(ang)], axis=1)
    return jnp.asarray(cos, dtype=F32), jnp.asarray(sin, dtype=F32)


def kernel(x, norm_ffn1, ffn1_w1, ffn1_w3, ffn1_w2, norm_mix, w_in, gla_gate_w2, gla_gate_b, gla_norm_w,
           w_branch_ret, w_branch_gla, w_out, norm_ffn2, ffn2_w1, ffn2_w3, ffn2_w2, norm_final):
    batch, seq, _ = x.shape
    depth = norm_ffn1.shape[0]
    t = batch * seq
    h = x.reshape(t, D_MODEL)
    cos_t, sin_t = _rope_tables(seq)
    row = lambda a: a.reshape(1, -1).astype(F32)
    bf = lambda a: a.astype(BF16)
    assert depth == 1, "the block's final RMSNorm is fused into the layer's closing FFN call"
    for l in range(depth):
        cols, off = [], 0
        for width in IN_SPLITS:
            cols.append(w_in[l][:, off:off + width])
            off += width
        wqa, wka, wva, wra, wqb, wkb, wvb, wrb, wlr, wga, wgb = cols
        wlr_p = jnp.pad(wlr, ((0, 0), (0, LANES - GATE_RANK))).astype(BF16)
        w2h = gla_gate_w2[l].astype(BF16)
        w2l = (gla_gate_w2[l] - w2h.astype(F32)).astype(BF16)
        w2p = jnp.pad(jnp.concatenate([w2h, w2h, w2l], axis=0), ((0, LANES - 3 * GATE_RANK), (0, 0)))
        proj_w = (bf(wqa), bf(wka), bf(wva), bf(wqb), bf(wkb), bf(wvb), wlr_p, w2p)
        h, qa, ka, va, qb, kb, vb, la = _ffn_proj(
            h, row(norm_ffn1[l]), bf(ffn1_w1[l]), bf(ffn1_w3[l]), bf(ffn1_w2[l]), row(norm_mix[l]),
            proj_w, row(gla_gate_b[l]), cos_t, sin_t, seq)
        h = _mix(qa, ka, va, qb, kb, vb, la, h, row(norm_mix[l]), bf(wra), bf(wrb), bf(wga), bf(wgb),
                 bf(w_branch_ret[l]), bf(w_branch_gla[l]), bf(w_out[l]), row(gla_norm_w[l]), seq)
        h = _ffn_final(h, row(norm_ffn2[l]), bf(ffn2_w1[l]), bf(ffn2_w3[l]), bf(ffn2_w2[l]), row(norm_final))
    return h.reshape(batch, seq, D_MODEL)
```

```python
import functools
import math

import numpy as np
import jax
import jax.numpy as jnp
from jax import lax
from jax.experimental import pallas as pl
from jax.experimental.pallas import tpu as pltpu

D_MODEL = 1024
CHUNK = 64
HEADS = 4
DK = 128
DV = 256
QK = HEADS * DK
VW = HEADS * DV
GATE_RANK = 16
GATE_TAU = 16.0
D_FF = 2816
ROPE_BASE = 10000.0
NORM_EPS = 1e-6
IN_SPLITS = (QK, QK, VW, VW, QK, QK, VW, VW, GATE_RANK, D_MODEL, D_MODEL)

LANES = 128
TM = 512
TM_FINAL = 1024
LBLK = 256
GRP = LBLK // 2
N_LEVELS = LBLK.bit_length() - 1
N_FINE = 2
TBLK = 512
FF_CHUNKS = ((0, 1024), (1024, 2048), (2048, D_FF))
VMEM_LIMIT = 52 * 1024 * 1024
MIX_VMEM_LIMIT = 60 * 1024 * 1024

F32 = jnp.float32
BF16 = jnp.bfloat16


def _const_spec(shape):
    zeros = (0,) * len(shape)
    return pl.BlockSpec(shape, lambda *_: zeros, pipeline_mode=pl.Buffered(1))


def _rms(x, w):
    ms = jnp.mean(x * x, axis=-1, keepdims=True)
    return x * lax.rsqrt(ms + NORM_EPS) * w


def _dot(a, b):
    return jnp.dot(a, b, preferred_element_type=F32)


def _dot_nt(a, b):
    return lax.dot_general(a, b, (((1,), (1,)), ((), ())), preferred_element_type=F32)


def _dot_tn(a, b):
    return lax.dot_general(a, b, (((0,), (0,)), ((), ())), preferred_element_type=F32)


def _sigmoid(x):
    return 1.0 / (1.0 + jnp.exp(-x))


SLAB = TM // 2


def _swiglu_half_step(x, nw_ref, w1_ref, w3_ref, w2_ref):
    xn = _rms(x, nw_ref[...]).astype(BF16)
    acc = None
    for a, b in FF_CHUNKS:
        a1 = _dot(xn, w1_ref[:, a:b])
        a3 = _dot(xn, w3_ref[:, a:b])
        g = (a1 * _sigmoid(a1) * a3).astype(BF16)
        p = _dot(g, w2_ref[a:b, :])
        acc = p if acc is None else acc + p
    return x + 0.5 * acc


def _proj_slab(h, rows, nw_ref, wqa, wka, wva, wqb, wkb, wvb, wlr, w2p, gb_ref, cos_ref, sin_ref,
               qa_o, ka_o, va_o, qb_o, kb_o, vb_o, la_o):
    scale = DK ** -0.5
    u = _rms(h, nw_ref[...]).astype(BF16)
    cos = cos_ref[rows, :]
    sin = sin_ref[rows, :]

    def rotary(x, o_ref, mul):
        for hd in range(HEADS):
            xh = x[:, hd * DK:(hd + 1) * DK]
            r = xh * cos + pltpu.roll(xh, DK // 2, axis=1) * sin
            if mul != 1.0:
                r = r * mul
            o_ref[rows, hd * DK:(hd + 1) * DK] = r.astype(BF16)

    lr = _dot(u, wlr[...])
    lr_hi = lr.astype(BF16).astype(F32)
    lr_lo = (lr - lr_hi).astype(BF16).astype(F32)
    packed = lr_hi + pltpu.roll(lr_lo, GATE_RANK, axis=1) + pltpu.roll(lr_hi, 2 * GATE_RANK, axis=1)
    z = _dot(packed.astype(BF16), w2p[...]) + gb_ref[...]
    rotary(_dot(u, wqa[...]), qa_o, scale)
    rotary(_dot(u, wka[...]), ka_o, 1.0)
    log_sig = jnp.minimum(z, 0.0) - jnp.log(1.0 + jnp.exp(-jnp.abs(z)))
    la_o[rows, :] = log_sig * (1.0 / GATE_TAU)
    va_o[rows, :] = _dot(u, wva[...]).astype(BF16)
    qb_o[rows, :] = (_dot(u, wqb[...]) * scale).astype(BF16)
    kb_o[rows, :] = _dot(u, wkb[...]).astype(BF16)
    vb_o[rows, :] = _dot(u, wvb[...]).astype(BF16)


def _ffn_kernel(x_ref, nw_ref, w1_ref, w3_ref, w2_ref, fw_ref, o_ref):
    for r0 in range(0, TM_FINAL, SLAB):
        rows = slice(r0, r0 + SLAB)
        h = _swiglu_half_step(x_ref[rows, :], nw_ref, w1_ref, w3_ref, w2_ref)
        o_ref[rows, :] = _rms(h, fw_ref[...])


def _ffn_proj_kernel(x_ref, nw_ref, w1_ref, w3_ref, w2_ref, nwm_ref, *rest):
    proj_in, (h_o, *proj_out) = rest[:11], rest[11:]
    for r0 in range(0, TM, SLAB):
        rows = slice(r0, r0 + SLAB)
        h = _swiglu_half_step(x_ref[rows, :], nw_ref, w1_ref, w3_ref, w2_ref)
        h_o[rows, :] = h
        _proj_slab(h, rows, nwm_ref, *proj_in, *proj_out)


def _ffn_specs(tm):
    tile = pl.BlockSpec((tm, D_MODEL), lambda i: (i, 0))
    return tile, [tile, _const_spec((1, D_MODEL)), _const_spec((D_MODEL, D_FF)),
                  _const_spec((D_MODEL, D_FF)), _const_spec((D_FF, D_MODEL)), _const_spec((1, D_MODEL))]


def _ffn_final(x, nw, w1, w3, w2, final_w):
    t = x.shape[0]
    tile, in_specs = _ffn_specs(TM_FINAL)
    return pl.pallas_call(
        _ffn_kernel,
        grid=(t // TM_FINAL,),
        in_specs=in_specs,
        out_specs=tile,
        out_shape=jax.ShapeDtypeStruct((t, D_MODEL), F32),
        compiler_params=pltpu.CompilerParams(
            dimension_semantics=("parallel",), vmem_limit_bytes=VMEM_LIMIT),
        name="ffn_final",
    )(x, nw, w1, w3, w2, final_w)


def _ffn_proj(x, nw, w1, w3, w2, nw_mix, proj_w, gate_b, cos_t, sin_t, seq):
    t = x.shape[0]
    tiles_per_seq = seq // TM
    tile, in_specs = _ffn_specs(TM)

    def cols(w):
        return pl.BlockSpec((TM, w), lambda i: (i, 0))

    pos_tile = pl.BlockSpec((TM, DK), lambda i: (i % tiles_per_seq, 0))
    in_specs += [_const_spec((D_MODEL, QK)), _const_spec((D_MODEL, QK)), _const_spec((D_MODEL, VW)),
                 _const_spec((D_MODEL, QK)), _const_spec((D_MODEL, QK)), _const_spec((D_MODEL, VW)),
                 _const_spec((D_MODEL, LANES)), _const_spec((LANES, QK)), _const_spec((1, QK)),
                 pos_tile, pos_tile]
    out_specs = [tile, cols(QK), cols(QK), cols(VW), cols(QK), cols(QK), cols(VW), cols(QK)]
    out_shape = [jax.ShapeDtypeStruct((t, D_MODEL), F32)] + [
        jax.ShapeDtypeStruct((t, w), d) for w, d in
        ((QK, BF16), (QK, BF16), (VW, BF16), (QK, BF16), (QK, BF16), (VW, BF16), (QK, F32))]
    return pl.pallas_call(
        _ffn_proj_kernel,
        grid=(t // TM,),
        in_specs=in_specs,
        out_specs=out_specs,
        out_shape=out_shape,
        compiler_params=pltpu.CompilerParams(
            dimension_semantics=("parallel",), vmem_limit_bytes=VMEM_LIMIT),
        name="ffn_proj",
    )(x, nw, w1, w3, w2, nw_mix, *proj_w, gate_b, cos_t, sin_t)


def _ret_tables():
    idx = np.arange(LBLK, dtype=np.float64)
    chunk = np.arange(LBLK) // CHUNK
    diff = idx[:, None] - idx[None, :]
    same = chunk[:, None] == chunk[None, :]
    earlier = chunk[None, :] < chunk[:, None]
    dmask, qdec, kdec, cdec = [], [], [], []
    for h in range(HEADS):
        lg = math.log1p(-2.0 ** (-5.0 - h))
        dmask.append(np.where(same, np.exp(lg * np.abs(diff)),
                              np.where(earlier, np.exp(lg * diff), 0.0)))
        qdec.append(np.broadcast_to(np.exp(lg * (idx + 1.0))[:, None], (LBLK, LANES)))
        kdec.append(np.broadcast_to(np.exp(lg * (LBLK - 1.0 - idx))[:, None], (LBLK, LANES)))
        cdec.append(math.exp(lg * LBLK))
    f = lambda a: jnp.asarray(np.stack(a), dtype=F32)
    return f(dmask), f(qdec), f(kdec), tuple(cdec)


def _gla_tables():
    n = np.arange(LBLK)
    mats = [n[None, :] <= n[:, None]]
    for p in range(N_FINE):
        ref = (n >> (p + 1) << (p + 1)) + (1 << p) - 1
        lo, hi = np.minimum(n, ref), np.maximum(n, ref)
        mats.append((n[None, :] > lo[:, None]) & (n[None, :] <= hi[:, None]))
    cum = np.concatenate(mats, axis=0).astype(np.float32)
    cum = np.concatenate([cum, cum], axis=1)
    m = np.arange(GRP)
    x = m[:, None] ^ m[None, :]
    lvl = np.where(x == 0, -1, np.floor(np.log2(np.maximum(x, 1))).astype(np.int64))
    hidden = (m[None, :] // CHUNK) > (m[:, None] // CHUNK)
    lvl = np.where(hidden, -2, lvl)
    second = ((n[None, :] >> np.arange(N_FINE, N_LEVELS)[:, None]) & 1).astype(np.float64)
    sgn = (2.0 * second - 1.0) * math.log2(math.e)
    sgn = np.broadcast_to(sgn[:, :, None], (N_LEVELS - N_FINE, LBLK, LANES))
    tri = np.where(m[None, :] <= m[:, None], 1, np.where(hidden, 0, 2))
    return (jnp.asarray(cum, dtype=BF16), jnp.asarray(lvl, dtype=jnp.int32),
            jnp.asarray(sgn, dtype=F32), jnp.asarray(tri, dtype=jnp.int32))


def _split_ref_rows(g, half):
    n = g.shape[0]
    size = 2 * half
    g3 = g.reshape(n // size, size, g.shape[1])
    ref = jnp.broadcast_to(g3[:, half - 1:half, :], g3.shape)
    return ref.reshape(g.shape)


GLA_MILD_DECAY = 60.0


def _ret_block(q_ref, k_ref, v_ref, rows, dmask_ref, qdec_ref, kdec_ref, cdec, state_ref, o_ref, slot):
    for h in range(HEADS):
        qk_cols = slice(h * DK, (h + 1) * DK)
        v_cols = slice(h * DV, (h + 1) * DV)
        q = q_ref[rows, qk_cols]
        k = k_ref[rows, qk_cols]
        v = v_ref[rows, v_cols]
        st = state_ref[h]
        s = _dot_nt(q, k) * dmask_ref[h]
        qdec = qdec_ref[h]
        inter = _dot(q, st.astype(BF16)) * jnp.concatenate([qdec, qdec], axis=1)
        o = _dot(s.astype(BF16), v) + inter
        kd = (k.astype(F32) * kdec_ref[h]).astype(BF16)
        state_ref[h] = cdec[h] * st + _dot_tn(kd, v)
        mu = jnp.mean(o, axis=-1, keepdims=True)
        oc = o - mu
        var = jnp.mean(oc * oc, axis=-1, keepdims=True)
        o_ref[slot, rows, v_cols] = (oc * lax.rsqrt(var + NORM_EPS)).astype(BF16)


def _gla_cum(la_ref, rows, cum_ref):
    la = la_ref[rows, :]
    la_hi = la.astype(BF16)
    la_lo = (la - la_hi.astype(F32)).astype(BF16)
    la2 = jnp.concatenate([la_hi, la_lo], axis=0)
    return la2, _dot(cum_ref[0:LBLK, :], la2)


def _gla_finish(h, q, k, v, g, s_top, s_bot_row, state_ref, nw_ref, o_ref, slot, rows):
    top = slice(0, GRP)
    v_cols = slice(h * DV, (h + 1) * DV)
    st = state_ref[h]
    inter = _dot_nt((q * jnp.exp(g)).astype(BF16), st.astype(BF16))
    o_top = _dot(s_top.astype(BF16), v[top])
    o_bot = _dot(s_bot_row.astype(BF16), v)
    o = jnp.concatenate([o_top, o_bot], axis=0) + inter
    g_last = g[LBLK - 1:LBLK, :]
    kd = (k * jnp.exp(g_last - g)).astype(BF16)
    state_ref[h] = st * jnp.exp(g_last) + _dot_tn(v, kd)
    ms = jnp.mean(o * o, axis=-1, keepdims=True)
    o_ref[slot, rows, v_cols] = (o * lax.rsqrt(ms + NORM_EPS) * nw_ref[...]).astype(BF16)


def _gla_block(q_ref, k_ref, v_ref, rows, la2, g_all, cum_ref, lvl_ref, sgn_ref, nw_ref, state_ref,
               o_ref, slot):
    top, bot = slice(0, GRP), slice(GRP, LBLK)
    lvl = lvl_ref[...]
    fine = _dot(cum_ref[LBLK:, :], la2)
    for h in range(HEADS):
        qk_cols = slice(h * DK, (h + 1) * DK)
        g = g_all[:, qk_cols]
        qb = q_ref[rows, qk_cols]
        kb = k_ref[rows, qk_cols]
        v = v_ref[rows, h * DV:(h + 1) * DV]
        q = qb.astype(F32)
        k = kb.astype(F32)
        s_grp = [jnp.where(lvl == -1, _dot_nt(qb[r], kb[r]), 0.0) for r in (top, bot)]
        s_cross = None
        for p in range(N_LEVELS):
            if p < N_FINE:
                e = jnp.exp(fine[p * LBLK:(p + 1) * LBLK, qk_cols])
            else:
                e = jnp.exp2((g - _split_ref_rows(g, 1 << p)) * sgn_ref[p - N_FINE])
            qe = (q * e).astype(BF16)
            ke = (k * e).astype(BF16)
            if (2 << p) <= GRP:
                s_grp = [jnp.where(lvl == p, _dot_nt(qe[r], ke[r]), s)
                         for r, s in zip((top, bot), s_grp)]
            else:
                s_cross = _dot_nt(qe[bot], ke[top])
        _gla_finish(h, q, k, v, g, s_grp[0], jnp.concatenate([s_cross, s_grp[1]], axis=1),
                    state_ref, nw_ref, o_ref, slot, rows)


def _gla_block_mild(q_ref, k_ref, v_ref, rows, g_all, tri_ref, nw_ref, state_ref, o_ref, slot):
    top, bot = slice(0, GRP), slice(GRP, LBLK)
    tri = tri_ref[...]
    for h in range(HEADS):
        qk_cols = slice(h * DK, (h + 1) * DK)
        g = g_all[:, qk_cols]
        q = q_ref[rows, qk_cols].astype(F32)
        k = k_ref[rows, qk_cols].astype(F32)
        v = v_ref[rows, h * DV:(h + 1) * DV]
        g0 = g[0:1, :]
        shrink = jnp.exp(g - g0)
        grow = jnp.exp(g0 - g)
        q_s, q_g = (q * shrink).astype(BF16), (q * grow).astype(BF16)
        k_s, k_g = (k * shrink).astype(BF16), (k * grow).astype(BF16)

        def group_scores(r):
            after = _dot_nt(q_s[r], k_g[r])
            before = _dot_nt(q_g[r], k_s[r])
            return jnp.where(tri == 1, after, jnp.where(tri == 2, before, 0.0))

        s_bot_row = jnp.concatenate([_dot_nt(q_s[bot], k_g[top]), group_scores(bot)], axis=1)
        _gla_finish(h, q, k, v, g, group_scores(top), s_bot_row, state_ref, nw_ref, o_ref, slot, rows)


def _mix_kernel(qa_ref, ka_ref, va_ref, qb_ref, kb_ref, vb_ref, la_ref, h_ref, nw_ref,
                wra, wrb, wga, wgb, wa, wb, wo,
                dmask_ref, qdec_ref, kdec_ref, cum_ref, lvl_ref, sgn_ref, tri_ref, gnw_ref,
                o_ref, oa_scr, ob_scr, ret_state, gla_state, *, cdec, blocks_per_seq, n_blocks):
    s = pl.program_id(0)
    blk = jnp.minimum(s, n_blocks - 1)
    slot = lax.rem(s, 2)
    prev = 1 - slot
    blocks = [slice(i * LBLK, (i + 1) * LBLK) for i in range(TBLK // LBLK)]

    @pl.when(s == 0)
    def _():
        oa_scr[...] = jnp.zeros_like(oa_scr)
        ob_scr[...] = jnp.zeros_like(ob_scr)

    @pl.when(lax.rem(blk, blocks_per_seq) == 0)
    def _():
        ret_state[...] = jnp.zeros_like(ret_state)
        gla_state[...] = jnp.zeros_like(gla_state)

    cums = [_gla_cum(la_ref, rows, cum_ref) for rows in blocks]
    decay = [jnp.max(g[0:1, :] - g[LBLK - 1:LBLK, :]) for _, g in cums]
    mild = functools.reduce(jnp.maximum, decay) < GLA_MILD_DECAY

    @pl.when(mild)
    def _():
        for rows, (_, g) in zip(blocks, cums):
            _gla_block_mild(qb_ref, kb_ref, vb_ref, rows, g, tri_ref, gnw_ref, gla_state, ob_scr, slot)

    @pl.when(jnp.logical_not(mild))
    def _():
        for rows, (la2, g) in zip(blocks, cums):
            _gla_block(qb_ref, kb_ref, vb_ref, rows, la2, g, cum_ref, lvl_ref, sgn_ref, gnw_ref, gla_state,
                       ob_scr, slot)

    for rows in blocks:
        _ret_block(qa_ref, ka_ref, va_ref, rows, dmask_ref, qdec_ref, kdec_ref, cdec, ret_state, oa_scr, slot)

    for r0 in range(0, TBLK, SLAB):
        rows = slice(r0, r0 + SLAB)
        h = h_ref[rows, :]
        u = _rms(h, nw_ref[...]).astype(BF16)

        def branch(o_scr, w_gate_out, w_branch, w_gate_merge):
            r = _dot(u, w_gate_out[...])
            og = (r * _sigmoid(r) * o_scr[prev, rows, :].astype(F32)).astype(BF16)
            return _sigmoid(_dot(u, w_gate_merge[...])) * _dot(og, w_branch[...])

        merged = branch(oa_scr, wra, wa, wga) + branch(ob_scr, wrb, wb, wgb)
        o_ref[rows, :] = h + _dot(merged.astype(BF16), wo[...])


def _mix(qa, ka, va, qb, kb, vb, la, h1, nw, wra, wrb, wga, wgb, wa, wb, wo, gla_nw, seq):
    t = h1.shape[0]
    n_blocks = t // TBLK
    dmask, qdec, kdec, cdec = _ret_tables()
    cum, lvl, sgn, tri = _gla_tables()

    def cur(w):
        return pl.BlockSpec((TBLK, w), lambda s: (jnp.minimum(s, n_blocks - 1), 0))

    prev_tile = pl.BlockSpec((TBLK, D_MODEL), lambda s: (jnp.maximum(s - 1, 0), 0))
    sq = _const_spec((D_MODEL, D_MODEL))
    return pl.pallas_call(
        functools.partial(_mix_kernel, cdec=cdec, blocks_per_seq=seq // TBLK, n_blocks=n_blocks),
        grid=(n_blocks + 1,),
        in_specs=[cur(QK), cur(QK), cur(VW), cur(QK), cur(QK), cur(VW), cur(QK), prev_tile,
                  _const_spec((1, D_MODEL)), sq, sq, sq, sq, sq, sq, sq,
                  _const_spec((HEADS, LBLK, LBLK)), _const_spec((HEADS, LBLK, LANES)),
                  _const_spec((HEADS, LBLK, LANES)), _const_spec(((1 + N_FINE) * LBLK, 2 * LBLK)),
                  _const_spec((GRP, GRP)), _const_spec((N_LEVELS - N_FINE, LBLK, LANES)),
                  _const_spec((GRP, GRP)), _const_spec((1, DV))],
        out_specs=prev_tile,
        out_shape=jax.ShapeDtypeStruct((t, D_MODEL), F32),
        scratch_shapes=[pltpu.VMEM((2, TBLK, VW), BF16), pltpu.VMEM((2, TBLK, VW), BF16),
                        pltpu.VMEM((HEADS, DK, DV), F32), pltpu.VMEM((HEADS, DV, DK), F32)],
        compiler_params=pltpu.CompilerParams(
            dimension_semantics=("arbitrary",), vmem_limit_bytes=MIX_VMEM_LIMIT),
        name="mix",
    )(qa, ka, va, qb, kb, vb, la, h1, nw, wra, wrb, wga, wgb, wa, wb, wo,
      dmask, qdec, kdec, cum, lvl, sgn, tri, gla_nw)


def _rope_tables(seq):
    half = DK // 2
    inv_freq = ROPE_BASE ** (-np.arange(half, dtype=np.float64) / half)
    ang = np.arange(seq, dtype=np.float64)[:, None] * inv_freq[None, :]
    cos = np.concatenate([np.cos(ang), np.cos(ang)], axis=1)
    sin = np.concatenate([-np.sin(ang), np.sin(ang)], axis=1)
    return jnp.asarray(cos, dtype=F32), jnp.asarray(sin, dtype=F32)


def kernel(x, norm_ffn1, ffn1_w1, ffn1_w3, ffn1_w2, norm_mix, w_in, gla_gate_w2, gla_gate_b, gla_norm_w,
           w_branch_ret, w_branch_gla, w_out, norm_ffn2, ffn2_w1, ffn2_w3, ffn2_w2, norm_final):
    batch, seq, _ = x.shape
    depth = norm_ffn1.shape[0]
    t = batch * seq
    h = x.reshape(t, D_MODEL)
    cos_t, sin_t = _rope_tables(seq)
    row = lambda a: a.reshape(1, -1).astype(F32)
    bf = lambda a: a.astype(BF16)
    assert depth == 1, "the block's final RMSNorm is fused into the layer's closing FFN call"
    for l in range(depth):
        cols, off = [], 0
        for width in IN_SPLITS:
            cols.append(w_in[l][:, off:off + width])
            off += width
        wqa, wka, wva, wra, wqb, wkb, wvb, wrb, wlr, wga, wgb = cols
        wlr_p = jnp.pad(wlr, ((0, 0), (0, LANES - GATE_RANK))).astype(BF16)
        w2h = gla_gate_w2[l].astype(BF16)
        w2l = (gla_gate_w2[l] - w2h.astype(F32)).astype(BF16)
        w2p = jnp.pad(jnp.concatenate([w2h, w2h, w2l], axis=0), ((0, LANES - 3 * GATE_RANK), (0, 0)))
        proj_w = (bf(wqa), bf(wka), bf(wva), bf(wqb), bf(wkb), bf(wvb), wlr_p, w2p)
        h, qa, ka, va, qb, kb, vb, la = _ffn_proj(
            h, row(norm_ffn1[l]), bf(ffn1_w1[l]), bf(ffn1_w3[l]), bf(ffn1_w2[l]), row(norm_mix[l]),
            proj_w, row(gla_gate_b[l]), cos_t, sin_t, seq)
        h = _mix(qa, ka, va, qb, kb, vb, la, h, row(norm_mix[l]), bf(wra), bf(wrb), bf(wga), bf(wgb),
                 bf(w_branch_ret[l]), bf(w_branch_gla[l]), bf(w_out[l]), row(gla_norm_w[l]), seq)
        h = _ffn_final(h, row(norm_ffn2[l]), bf(ffn2_w1[l]), bf(ffn2_w3[l]), bf(ffn2_w2[l]), row(norm_final))
    return h.reshape(batch, seq, D_MODEL)
```

```python
import functools
import math

import numpy as np
import jax
import jax.numpy as jnp
from jax import lax
from jax.experimental import pallas as pl
from jax.experimental.pallas import tpu as pltpu

D_MODEL = 1024
CHUNK = 64
HEADS = 4
DK = 128
DV = 256
QK = HEADS * DK
VW = HEADS * DV
GATE_RANK = 16
GATE_TAU = 16.0
D_FF = 2816
ROPE_BASE = 10000.0
NORM_EPS = 1e-6
IN_SPLITS = (QK, QK, VW, VW, QK, QK, VW, VW, GATE_RANK, D_MODEL, D_MODEL)

LANES = 128
TM = 512
TM_FINAL = 1024
LBLK = 256
GRP = LBLK // 2
N_LEVELS = LBLK.bit_length() - 1
N_FINE = 2
TBLK = 512
FF_CHUNKS = ((0, 1024), (1024, 2048), (2048, D_FF))
VMEM_LIMIT = 52 * 1024 * 1024
MIX_VMEM_LIMIT = 60 * 1024 * 1024

F32 = jnp.float32
BF16 = jnp.bfloat16


def _const_spec(shape):
    zeros = (0,) * len(shape)
    return pl.BlockSpec(shape, lambda *_: zeros, pipeline_mode=pl.Buffered(1))


def _rms(x, w):
    ms = jnp.mean(x * x, axis=-1, keepdims=True)
    return x * lax.rsqrt(ms + NORM_EPS) * w


def _dot(a, b):
    return jnp.dot(a, b, preferred_element_type=F32)


def _dot_nt(a, b):
    return lax.dot_general(a, b, (((1,), (1,)), ((), ())), preferred_element_type=F32)


def _dot_tn(a, b):
    return lax.dot_general(a, b, (((0,), (0,)), ((), ())), preferred_element_type=F32)


def _sigmoid(x):
    return 1.0 / (1.0 + jnp.exp(-x))


SLAB = TM // 2


def _swiglu_half_step(x, nw_ref, w1_ref, w3_ref, w2_ref):
    xn = _rms(x, nw_ref[...]).astype(BF16)
    acc = None
    for a, b in FF_CHUNKS:
        a1 = _dot(xn, w1_ref[:, a:b])
        a3 = _dot(xn, w3_ref[:, a:b])
        g = (a1 * _sigmoid(a1) * a3).astype(BF16)
        p = _dot(g, w2_ref[a:b, :])
        acc = p if acc is None else acc + p
    return x + 0.5 * acc


def _proj_slab(h, rows, nw_ref, wqa, wka, wva, wqb, wkb, wvb, wlr, w2p, gb_ref, cos_ref, sin_ref,
               qa_o, ka_o, va_o, qb_o, kb_o, vb_o, la_o):
    scale = DK ** -0.5
    u = _rms(h, nw_ref[...]).astype(BF16)
    cos = cos_ref[rows, :]
    sin = sin_ref[rows, :]

    def rotary(x, o_ref, mul):
        for hd in range(HEADS):
            xh = x[:, hd * DK:(hd + 1) * DK]
            r = xh * cos + pltpu.roll(xh, DK // 2, axis=1) * sin
            if mul != 1.0:
                r = r * mul
            o_ref[rows, hd * DK:(hd + 1) * DK] = r.astype(BF16)

    lr = _dot(u, wlr[...])
    lr_hi = lr.astype(BF16).astype(F32)
    lr_lo = (lr - lr_hi).astype(BF16).astype(F32)
    packed = lr_hi + pltpu.roll(lr_lo, GATE_RANK, axis=1) + pltpu.roll(lr_hi, 2 * GATE_RANK, axis=1)
    z = _dot(packed.astype(BF16), w2p[...]) + gb_ref[...]
    rotary(_dot(u, wqa[...]), qa_o, scale)
    rotary(_dot(u, wka[...]), ka_o, 1.0)
    log_sig = jnp.minimum(z, 0.0) - jnp.log(1.0 + jnp.exp(-jnp.abs(z)))
    la_o[rows, :] = log_sig * (1.0 / GATE_TAU)
    va_o[rows, :] = _dot(u, wva[...]).astype(BF16)
    qb_o[rows, :] = (_dot(u, wqb[...]) * scale).astype(BF16)
    kb_o[rows, :] = _dot(u, wkb[...]).astype(BF16)
    vb_o[rows, :] = _dot(u, wvb[...]).astype(BF16)


def _ffn_kernel(x_ref, nw_ref, w1_ref, w3_ref, w2_ref, fw_ref, o_ref):
    for r0 in range(0, TM_FINAL, SLAB):
        rows = slice(r0, r0 + SLAB)
        h = _swiglu_half_step(x_ref[rows, :], nw_ref, w1_ref, w3_ref, w2_ref)
        o_ref[rows, :] = _rms(h, fw_ref[...])


def _ffn_proj_kernel(x_ref, nw_ref, w1_ref, w3_ref, w2_ref, nwm_ref, *rest):
    proj_in, (h_o, *proj_out) = rest[:11], rest[11:]
    for r0 in range(0, TM, SLAB):
        rows = slice(r0, r0 + SLAB)
        h = _swiglu_half_step(x_ref[rows, :], nw_ref, w1_ref, w3_ref, w2_ref)
        h_o[rows, :] = h
        _proj_slab(h, rows, nwm_ref, *proj_in, *proj_out)


def _ffn_specs(tm):
    tile = pl.BlockSpec((tm, D_MODEL), lambda i: (i, 0))
    return tile, [tile, _const_spec((1, D_MODEL)), _const_spec((D_MODEL, D_FF)),
                  _const_spec((D_MODEL, D_FF)), _const_spec((D_FF, D_MODEL)), _const_spec((1, D_MODEL))]


def _ffn_final(x, nw, w1, w3, w2, final_w):
    t = x.shape[0]
    tile, in_specs = _ffn_specs(TM_FINAL)
    return pl.pallas_call(
        _ffn_kernel,
        grid=(t // TM_FINAL,),
        in_specs=in_specs,
        out_specs=tile,
        out_shape=jax.ShapeDtypeStruct((t, D_MODEL), F32),
        compiler_params=pltpu.CompilerParams(
            dimension_semantics=("parallel",), vmem_limit_bytes=VMEM_LIMIT),
        name="ffn_final",
    )(x, nw, w1, w3, w2, final_w)


def _ffn_proj(x, nw, w1, w3, w2, nw_mix, proj_w, gate_b, cos_t, sin_t, seq):
    t = x.shape[0]
    tiles_per_seq = seq // TM
    tile, in_specs = _ffn_specs(TM)

    def cols(w):
        return pl.BlockSpec((TM, w), lambda i: (i, 0))

    pos_tile = pl.BlockSpec((TM, DK), lambda i: (i % tiles_per_seq, 0))
    in_specs += [_const_spec((D_MODEL, QK)), _const_spec((D_MODEL, QK)), _const_spec((D_MODEL, VW)),
                 _const_spec((D_MODEL, QK)), _const_spec((D_MODEL, QK)), _const_spec((D_MODEL, VW)),
                 _const_spec((D_MODEL, LANES)), _const_spec((LANES, QK)), _const_spec((1, QK)),
                 pos_tile, pos_tile]
    out_specs = [tile, cols(QK), cols(QK), cols(VW), cols(QK), cols(QK), cols(VW), cols(QK)]
    out_shape = [jax.ShapeDtypeStruct((t, D_MODEL), F32)] + [
        jax.ShapeDtypeStruct((t, w), d) for w, d in
        ((QK, BF16), (QK, BF16), (VW, BF16), (QK, BF16), (QK, BF16), (VW, BF16), (QK, F32))]
    return pl.pallas_call(
        _ffn_proj_kernel,
        grid=(t // TM,),
        in_specs=in_specs,
        out_specs=out_specs,
        out_shape=out_shape,
        compiler_params=pltpu.CompilerParams(
            dimension_semantics=("parallel",), vmem_limit_bytes=VMEM_LIMIT),
        name="ffn_proj",
    )(x, nw, w1, w3, w2, nw_mix, *proj_w, gate_b, cos_t, sin_t)


def _ret_tables():
    idx = np.arange(LBLK, dtype=np.float64)
    chunk = np.arange(LBLK) // CHUNK
    diff = idx[:, None] - idx[None, :]
    same = chunk[:, None] == chunk[None, :]
    earlier = chunk[None, :] < chunk[:, None]
    dmask, qdec, kdec, cdec = [], [], [], []
    for h in range(HEADS):
        lg = math.log1p(-2.0 ** (-5.0 - h))
        dmask.append(np.where(same, np.exp(lg * np.abs(diff)),
                              np.where(earlier, np.exp(lg * diff), 0.0)))
        qdec.append(np.broadcast_to(np.exp(lg * (idx + 1.0))[:, None], (LBLK, LANES)))
        kdec.append(np.broadcast_to(np.exp(lg * (LBLK - 1.0 - idx))[:, None], (LBLK, LANES)))
        cdec.append(math.exp(lg * LBLK))
    f = lambda a: jnp.asarray(np.stack(a), dtype=F32)
    return f(dmask), f(qdec), f(kdec), tuple(cdec)


def _gla_tables():
    n = np.arange(LBLK)
    mats = [n[None, :] <= n[:, None]]
    for p in range(N_FINE):
        ref = (n >> (p + 1) << (p + 1)) + (1 << p) - 1
        lo, hi = np.minimum(n, ref), np.maximum(n, ref)
        mats.append((n[None, :] > lo[:, None]) & (n[None, :] <= hi[:, None]))
    cum = np.concatenate(mats, axis=0).astype(np.float32)
    cum = np.concatenate([cum, cum], axis=1)
    m = np.arange(GRP)
    x = m[:, None] ^ m[None, :]
    lvl = np.where(x == 0, -1, np.floor(np.log2(np.maximum(x, 1))).astype(np.int64))
    hidden = (m[None, :] // CHUNK) > (m[:, None] // CHUNK)
    lvl = np.where(hidden, -2, lvl)
    second = ((n[None, :] >> np.arange(N_FINE, N_LEVELS)[:, None]) & 1).astype(np.float64)
    sgn = (2.0 * second - 1.0) * math.log2(math.e)
    sgn = np.broadcast_to(sgn[:, :, None], (N_LEVELS - N_FINE, LBLK, LANES))
    tri = np.where(m[None, :] <= m[:, None], 1, np.where(hidden, 0, 2))
    return (jnp.asarray(cum, dtype=BF16), jnp.asarray(lvl, dtype=jnp.int32),
            jnp.asarray(sgn, dtype=F32), jnp.asarray(tri, dtype=jnp.int32))


def _split_ref_rows(g, half):
    n = g.shape[0]
    size = 2 * half
    g3 = g.reshape(n // size, size, g.shape[1])
    ref = jnp.broadcast_to(g3[:, half - 1:half, :], g3.shape)
    return ref.reshape(g.shape)


GLA_MILD_DECAY = 60.0


def _ret_block(q_ref, k_ref, v_ref, rows, dmask_ref, qdec_ref, kdec_ref, cdec, state_ref, o_ref, slot):
    for h in range(HEADS):
        qk_cols = slice(h * DK, (h + 1) * DK)
        v_cols = slice(h * DV, (h + 1) * DV)
        q = q_ref[rows, qk_cols]
        k = k_ref[rows, qk_cols]
        v = v_ref[rows, v_cols]
        st = state_ref[h]
        s = _dot_nt(q, k) * dmask_ref[h]
        qdec = qdec_ref[h]
        inter = _dot(q, st.astype(BF16)) * jnp.concatenate([qdec, qdec], axis=1)
        o = _dot(s.astype(BF16), v) + inter
        kd = (k.astype(F32) * kdec_ref[h]).astype(BF16)
        state_ref[h] = cdec[h] * st + _dot_tn(kd, v)
        mu = jnp.mean(o, axis=-1, keepdims=True)
        oc = o - mu
        var = jnp.mean(oc * oc, axis=-1, keepdims=True)
        o_ref[slot, rows, v_cols] = (oc * lax.rsqrt(var + NORM_EPS)).astype(BF16)


def _gla_cum(la_ref, rows, cum_ref):
    la = la_ref[rows, :]
    la_hi = la.astype(BF16)
    la_lo = (la - la_hi.astype(F32)).astype(BF16)
    la2 = jnp.concatenate([la_hi, la_lo], axis=0)
    return la2, _dot(cum_ref[0:LBLK, :], la2)


def _gla_finish(h, q, k, v, g, s_top, s_bot_row, state_ref, nw_ref, o_ref, slot, rows):
    top = slice(0, GRP)
    v_cols = slice(h * DV, (h + 1) * DV)
    st = state_ref[h]
    inter = _dot_nt((q * jnp.exp(g)).astype(BF16), st.astype(BF16))
    o_top = _dot(s_top.astype(BF16), v[top])
    o_bot = _dot(s_bot_row.astype(BF16), v)
    o = jnp.concatenate([o_top, o_bot], axis=0) + inter
    g_last = g[LBLK - 1:LBLK, :]
    kd = (k * jnp.exp(g_last - g)).astype(BF16)
    state_ref[h] = st * jnp.exp(g_last) + _dot_tn(v, kd)
    ms = jnp.mean(o * o, axis=-1, keepdims=True)
    o_ref[slot, rows, v_cols] = (o * lax.rsqrt(ms + NORM_EPS) * nw_ref[...]).astype(BF16)


def _gla_block(q_ref, k_ref, v_ref, rows, la2, g_all, cum_ref, lvl_ref, sgn_ref, nw_ref, state_ref,
               o_ref, slot):
    top, bot = slice(0, GRP), slice(GRP, LBLK)
    lvl = lvl_ref[...]
    fine = _dot(cum_ref[LBLK:, :], la2)
    for h in range(HEADS):
        qk_cols = slice(h * DK, (h + 1) * DK)
        g = g_all[:, qk_cols]
        qb = q_ref[rows, qk_cols]
        kb = k_ref[rows, qk_cols]
        v = v_ref[rows, h * DV:(h + 1) * DV]
        q = qb.astype(F32)
        k = kb.astype(F32)
        s_grp = [jnp.where(lvl == -1, _dot_nt(qb[r], kb[r]), 0.0) for r in (top, bot)]
        s_cross = None
        for p in range(N_LEVELS):
            if p < N_FINE:
                e = jnp.exp(fine[p * LBLK:(p + 1) * LBLK, qk_cols])
            else:
                e = jnp.exp2((g - _split_ref_rows(g, 1 << p)) * sgn_ref[p - N_FINE])
            qe = (q * e).astype(BF16)
            ke = (k * e).astype(BF16)
            if (2 << p) <= GRP:
                s_grp = [jnp.where(lvl == p, _dot_nt(qe[r], ke[r]), s)
                         for r, s in zip((top, bot), s_grp)]
            else:
                s_cross = _dot_nt(qe[bot], ke[top])
        _gla_finish(h, q, k, v, g, s_grp[0], jnp.concatenate([s_cross, s_grp[1]], axis=1),
                    state_ref, nw_ref, o_ref, slot, rows)


def _gla_block_mild(q_ref, k_ref, v_ref, rows, g_all, tri_ref, nw_ref, state_ref, o_ref, slot):
    top, bot = slice(0, GRP), slice(GRP, LBLK)
    tri = tri_ref[...]
    for h in range(HEADS):
        qk_cols = slice(h * DK, (h + 1) * DK)
        g = g_all[:, qk_cols]
        q = q_ref[rows, qk_cols].astype(F32)
        k = k_ref[rows, qk_cols].astype(F32)
        v = v_ref[rows, h * DV:(h + 1) * DV]
        g0 = g[0:1, :]
        shrink = jnp.exp(g - g0)
        grow = jnp.exp(g0 - g)
        q_s, q_g = (q * shrink).astype(BF16), (q * grow).astype(BF16)
        k_s, k_g = (k * shrink).astype(BF16), (k * grow).astype(BF16)

        def group_scores(r):
            after = _dot_nt(q_s[r], k_g[r])
            before = _dot_nt(q_g[r], k_s[r])
            return jnp.where(tri == 1, after, jnp.where(tri == 2, before, 0.0))

        s_bot_row = jnp.concatenate([_dot_nt(q_s[bot], k_g[top]), group_scores(bot)], axis=1)
        _gla_finish(h, q, k, v, g, group_scores(top), s_bot_row, state_ref, nw_ref, o_ref, slot, rows)


def _mix_kernel(qa_ref, ka_ref, va_ref, qb_ref, kb_ref, vb_ref, la_ref, h_ref, nw_ref,
                wra, wrb, wga, wgb, wa, wb, wo,
                dmask_ref, qdec_ref, kdec_ref, cum_ref, lvl_ref, sgn_ref, tri_ref, gnw_ref,
                o_ref, oa_scr, ob_scr, ret_state, gla_state, *, cdec, blocks_per_seq, n_blocks):
    s = pl.program_id(0)
    blk = jnp.minimum(s, n_blocks - 1)
    slot = lax.rem(s, 2)
    prev = 1 - slot
    blocks = [slice(i * LBLK, (i + 1) * LBLK) for i in range(TBLK // LBLK)]

    @pl.when(s == 0)
    def _():
        oa_scr[...] = jnp.zeros_like(oa_scr)
        ob_scr[...] = jnp.zeros_like(ob_scr)

    @pl.when(lax.rem(blk, blocks_per_seq) == 0)
    def _():
        ret_state[...] = jnp.zeros_like(ret_state)
        gla_state[...] = jnp.zeros_like(gla_state)

    cums = [_gla_cum(la_ref, rows, cum_ref) for rows in blocks]
    decay = [jnp.max(g[0:1, :] - g[LBLK - 1:LBLK, :]) for _, g in cums]
    mild = functools.reduce(jnp.maximum, decay) < GLA_MILD_DECAY

    for rows in blocks:
        _ret_block(qa_ref, ka_ref, va_ref, rows, dmask_ref, qdec_ref, kdec_ref, cdec, ret_state, oa_scr, slot)

    for r0 in range(0, TBLK, SLAB):
        rows = slice(r0, r0 + SLAB)
        h = h_ref[rows, :]
        u = _rms(h, nw_ref[...]).astype(BF16)

        def branch(o_scr, w_gate_out, w_branch, w_gate_merge):
            r = _dot(u, w_gate_out[...])
            og = (r * _sigmoid(r) * o_scr[prev, rows, :].astype(F32)).astype(BF16)
            return _sigmoid(_dot(u, w_gate_merge[...])) * _dot(og, w_branch[...])

        merged = branch(oa_scr, wra, wa, wga) + branch(ob_scr, wrb, wb, wgb)
        o_ref[rows, :] = h + _dot(merged.astype(BF16), wo[...])

    @pl.when(mild)
    def _():
        for rows, (_, g) in zip(blocks, cums):
            _gla_block_mild(qb_ref, kb_ref, vb_ref, rows, g, tri_ref, gnw_ref, gla_state, ob_scr, slot)

    @pl.when(jnp.logical_not(mild))
    def _():
        for rows, (la2, g) in zip(blocks, cums):
            _gla_block(qb_ref, kb_ref, vb_ref, rows, la2, g, cum_ref, lvl_ref, sgn_ref, gnw_ref, gla_state,
                       ob_scr, slot)


def _mix(qa, ka, va, qb, kb, vb, la, h1, nw, wra, wrb, wga, wgb, wa, wb, wo, gla_nw, seq):
    t = h1.shape[0]
    n_blocks = t // TBLK
    dmask, qdec, kdec, cdec = _ret_tables()
    cum, lvl, sgn, tri = _gla_tables()

    def cur(w):
        return pl.BlockSpec((TBLK, w), lambda s: (jnp.minimum(s, n_blocks - 1), 0))

    prev_tile = pl.BlockSpec((TBLK, D_MODEL), lambda s: (jnp.maximum(s - 1, 0), 0))
    sq = _const_spec((D_MODEL, D_MODEL))
    return pl.pallas_call(
        functools.partial(_mix_kernel, cdec=cdec, blocks_per_seq=seq // TBLK, n_blocks=n_blocks),
        grid=(n_blocks + 1,),
        in_specs=[cur(QK), cur(QK), cur(VW), cur(QK), cur(QK), cur(VW), cur(QK), prev_tile,
                  _const_spec((1, D_MODEL)), sq, sq, sq, sq, sq, sq, sq,
                  _const_spec((HEADS, LBLK, LBLK)), _const_spec((HEADS, LBLK, LANES)),
                  _const_spec((HEADS, LBLK, LANES)), _const_spec(((1 + N_FINE) * LBLK, 2 * LBLK)),
                  _const_spec((GRP, GRP)), _const_spec((N_LEVELS - N_FINE, LBLK, LANES)),
                  _const_spec((GRP, GRP)), _const_spec((1, DV))],
        out_specs=prev_tile,
        out_shape=jax.ShapeDtypeStruct((t, D_MODEL), F32),
        scratch_shapes=[pltpu.VMEM((2, TBLK, VW), BF16), pltpu.VMEM((2, TBLK, VW), BF16),
                        pltpu.VMEM((HEADS, DK, DV), F32), pltpu.VMEM((HEADS, DV, DK), F32)],
        compiler_params=pltpu.CompilerParams(
            dimension_semantics=("arbitrary",), vmem_limit_bytes=MIX_VMEM_LIMIT),
        name="mix",
    )(qa, ka, va, qb, kb, vb, la, h1, nw, wra, wrb, wga, wgb, wa, wb, wo,
      dmask, qdec, kdec, cum, lvl, sgn, tri, gla_nw)


def _rope_tables(seq):
    half = DK // 2
    inv_freq = ROPE_BASE ** (-np.arange(half, dtype=np.float64) / half)
    ang = np.arange(seq, dtype=np.float64)[:, None] * inv_freq[None, :]
    cos = np.concatenate([np.cos(ang), np.cos(ang)], axis=1)
    sin = np.concatenate([-np.sin(ang), np.sin(ang)], axis=1)
    return jnp.asarray(cos, dtype=F32), jnp.asarray(sin, dtype=F32)


def kernel(x, norm_ffn1, ffn1_w1, ffn1_w3, ffn1_w2, norm_mix, w_in, gla_gate_w2, gla_gate_b, gla_norm_w,
           w_branch_ret, w_branch_gla, w_out, norm_ffn2, ffn2_w1, ffn2_w3, ffn2_w2, norm_final):
    batch, seq, _ = x.shape
    depth = norm_ffn1.shape[0]
    t = batch * seq
    h = x.reshape(t, D_MODEL)
    cos_t, sin_t = _rope_tables(seq)
    row = lambda a: a.reshape(1, -1).astype(F32)
    bf = lambda a: a.astype(BF16)
    assert depth == 1, "the block's final RMSNorm is fused into the layer's closing FFN call"
    for l in range(depth):
        cols, off = [], 0
        for width in IN_SPLITS:
            cols.append(w_in[l][:, off:off + width])
            off += width
        wqa, wka, wva, wra, wqb, wkb, wvb, wrb, wlr, wga, wgb = cols
        wlr_p = jnp.pad(wlr, ((0, 0), (0, LANES - GATE_RANK))).astype(BF16)
        w2h = gla_gate_w2[l].astype(BF16)
        w2l = (gla_gate_w2[l] - w2h.astype(F32)).astype(BF16)
        w2p = jnp.pad(jnp.concatenate([w2h, w2h, w2l], axis=0), ((0, LANES - 3 * GATE_RANK), (0, 0)))
        proj_w = (bf(wqa), bf(wka), bf(wva), bf(wqb), bf(wkb), bf(wvb), wlr_p, w2p)
        h, qa, ka, va, qb, kb, vb, la = _ffn_proj(
            h, row(norm_ffn1[l]), bf(ffn1_w1[l]), bf(ffn1_w3[l]), bf(ffn1_w2[l]), row(norm_mix[l]),
            proj_w, row(gla_gate_b[l]), cos_t, sin_t, seq)
        h = _mix(qa, ka, va, qb, kb, vb, la, h, row(norm_mix[l]), bf(wra), bf(wrb), bf(wga), bf(wgb),
                 bf(w_branch_ret[l]), bf(w_branch_gla[l]), bf(w_out[l]), row(gla_norm_w[l]), seq)
        h = _ffn_final(h, row(norm_ffn2[l]), bf(ffn2_w1[l]), bf(ffn2_w3[l]), bf(ffn2_w2[l]), row(norm_final))
    return h.reshape(batch, seq, D_MODEL)
```

```python
import functools
import math

import numpy as np
import jax
import jax.numpy as jnp
from jax import lax
from jax.experimental import pallas as pl
from jax.experimental.pallas import tpu as pltpu

D_MODEL = 1024
CHUNK = 64
HEADS = 4
DK = 128
DV = 256
QK = HEADS * DK
VW = HEADS * DV
GATE_RANK = 16
GATE_TAU = 16.0
D_FF = 2816
ROPE_BASE = 10000.0
NORM_EPS = 1e-6
IN_SPLITS = (QK, QK, VW, VW, QK, QK, VW, VW, GATE_RANK, D_MODEL, D_MODEL)

LANES = 128
TM = 512
TM_FINAL = 1024
LBLK = 256
GRP = LBLK // 2
N_LEVELS = LBLK.bit_length() - 1
N_FINE = 2
TBLK = 512
FF_CHUNKS = ((0, 1024), (1024, 2048), (2048, D_FF))
VMEM_LIMIT = 52 * 1024 * 1024
MIX_VMEM_LIMIT = 60 * 1024 * 1024

F32 = jnp.float32
BF16 = jnp.bfloat16


def _const_spec(shape):
    zeros = (0,) * len(shape)
    return pl.BlockSpec(shape, lambda *_: zeros, pipeline_mode=pl.Buffered(1))


def _rms(x, w):
    ms = jnp.mean(x * x, axis=-1, keepdims=True)
    return x * lax.rsqrt(ms + NORM_EPS) * w


def _dot(a, b):
    return jnp.dot(a, b, preferred_element_type=F32)


def _dot_nt(a, b):
    return lax.dot_general(a, b, (((1,), (1,)), ((), ())), preferred_element_type=F32)


def _dot_tn(a, b):
    return lax.dot_general(a, b, (((0,), (0,)), ((), ())), preferred_element_type=F32)


def _sigmoid(x):
    return 1.0 / (1.0 + jnp.exp(-x))


SLAB = TM // 2


def _swiglu_half_step(x, nw_ref, w1_ref, w3_ref, w2_ref):
    xn = _rms(x, nw_ref[...]).astype(BF16)
    acc = None
    for a, b in FF_CHUNKS:
        a1 = _dot(xn, w1_ref[:, a:b])
        a3 = _dot(xn, w3_ref[:, a:b])
        g = (a1 * _sigmoid(a1) * a3).astype(BF16)
        p = _dot(g, w2_ref[a:b, :])
        acc = p if acc is None else acc + p
    return x + 0.5 * acc


def _proj_slab(h, rows, nw_ref, wqa, wka, wva, wqb, wkb, wvb, wlr, w2p, gb_ref, cos_ref, sin_ref,
               qa_o, ka_o, va_o, qb_o, kb_o, vb_o, la_o):
    scale = DK ** -0.5
    u = _rms(h, nw_ref[...]).astype(BF16)
    cos = cos_ref[rows, :]
    sin = sin_ref[rows, :]

    def rotary(x, o_ref, mul):
        for hd in range(HEADS):
            xh = x[:, hd * DK:(hd + 1) * DK]
            r = xh * cos + pltpu.roll(xh, DK // 2, axis=1) * sin
            if mul != 1.0:
                r = r * mul
            o_ref[rows, hd * DK:(hd + 1) * DK] = r.astype(BF16)

    lr = _dot(u, wlr[...])
    lr_hi = lr.astype(BF16).astype(F32)
    lr_lo = (lr - lr_hi).astype(BF16).astype(F32)
    packed = lr_hi + pltpu.roll(lr_lo, GATE_RANK, axis=1) + pltpu.roll(lr_hi, 2 * GATE_RANK, axis=1)
    z = _dot(packed.astype(BF16), w2p[...]) + gb_ref[...]
    rotary(_dot(u, wqa[...]), qa_o, scale)
    rotary(_dot(u, wka[...]), ka_o, 1.0)
    log_sig = jnp.minimum(z, 0.0) - jnp.log(1.0 + jnp.exp(-jnp.abs(z)))
    la_o[rows, :] = log_sig * (1.0 / GATE_TAU)
    va_o[rows, :] = _dot(u, wva[...]).astype(BF16)
    qb_o[rows, :] = (_dot(u, wqb[...]) * scale).astype(BF16)
    kb_o[rows, :] = _dot(u, wkb[...]).astype(BF16)
    vb_o[rows, :] = _dot(u, wvb[...]).astype(BF16)


def _ffn_kernel(x_ref, nw_ref, w1_ref, w3_ref, w2_ref, fw_ref, o_ref):
    for r0 in range(0, TM_FINAL, SLAB):
        rows = slice(r0, r0 + SLAB)
        h = _swiglu_half_step(x_ref[rows, :], nw_ref, w1_ref, w3_ref, w2_ref)
        o_ref[rows, :] = _rms(h, fw_ref[...])


def _ffn_proj_kernel(x_ref, nw_ref, w1_ref, w3_ref, w2_ref, nwm_ref, *rest):
    proj_in, (h_o, *proj_out) = rest[:11], rest[11:]
    for r0 in range(0, TM, SLAB):
        rows = slice(r0, r0 + SLAB)
        h = _swiglu_half_step(x_ref[rows, :], nw_ref, w1_ref, w3_ref, w2_ref)
        h_o[rows, :] = h
        _proj_slab(h, rows, nwm_ref, *proj_in, *proj_out)


def _ffn_specs(tm):
    tile = pl.BlockSpec((tm, D_MODEL), lambda i: (i, 0))
    return tile, [tile, _const_spec((1, D_MODEL)), _const_spec((D_MODEL, D_FF)),
                  _const_spec((D_MODEL, D_FF)), _const_spec((D_FF, D_MODEL)), _const_spec((1, D_MODEL))]


def _ffn_final(x, nw, w1, w3, w2, final_w):
    t = x.shape[0]
    tile, in_specs = _ffn_specs(TM_FINAL)
    return pl.pallas_call(
        _ffn_kernel,
        grid=(t // TM_FINAL,),
        in_specs=in_specs,
        out_specs=tile,
        out_shape=jax.ShapeDtypeStruct((t, D_MODEL), F32),
        compiler_params=pltpu.CompilerParams(
            dimension_semantics=("parallel",), vmem_limit_bytes=VMEM_LIMIT),
        name="ffn_final",
    )(x, nw, w1, w3, w2, final_w)


def _ffn_proj(x, nw, w1, w3, w2, nw_mix, proj_w, gate_b, cos_t, sin_t, seq):
    t = x.shape[0]
    tiles_per_seq = seq // TM
    tile, in_specs = _ffn_specs(TM)

    def cols(w):
        return pl.BlockSpec((TM, w), lambda i: (i, 0))

    pos_tile = pl.BlockSpec((TM, DK), lambda i: (i % tiles_per_seq, 0))
    in_specs += [_const_spec((D_MODEL, QK)), _const_spec((D_MODEL, QK)), _const_spec((D_MODEL, VW)),
                 _const_spec((D_MODEL, QK)), _const_spec((D_MODEL, QK)), _const_spec((D_MODEL, VW)),
                 _const_spec((D_MODEL, LANES)), _const_spec((LANES, QK)), _const_spec((1, QK)),
                 pos_tile, pos_tile]
    out_specs = [tile, cols(QK), cols(QK), cols(VW), cols(QK), cols(QK), cols(VW), cols(QK)]
    out_shape = [jax.ShapeDtypeStruct((t, D_MODEL), F32)] + [
        jax.ShapeDtypeStruct((t, w), d) for w, d in
        ((QK, BF16), (QK, BF16), (VW, BF16), (QK, BF16), (QK, BF16), (VW, BF16), (QK, F32))]
    return pl.pallas_call(
        _ffn_proj_kernel,
        grid=(t // TM,),
        in_specs=in_specs,
        out_specs=out_specs,
        out_shape=out_shape,
        compiler_params=pltpu.CompilerParams(
            dimension_semantics=("parallel",), vmem_limit_bytes=VMEM_LIMIT),
        name="ffn_proj",
    )(x, nw, w1, w3, w2, nw_mix, *proj_w, gate_b, cos_t, sin_t)


def _ret_tables():
    idx = np.arange(LBLK, dtype=np.float64)
    chunk = np.arange(LBLK) // CHUNK
    diff = idx[:, None] - idx[None, :]
    same = chunk[:, None] == chunk[None, :]
    earlier = chunk[None, :] < chunk[:, None]
    dmask, qdec, kdec, cdec = [], [], [], []
    for h in range(HEADS):
        lg = math.log1p(-2.0 ** (-5.0 - h))
        dmask.append(np.where(same, np.exp(lg * np.abs(diff)),
                              np.where(earlier, np.exp(lg * diff), 0.0)))
        qdec.append(np.broadcast_to(np.exp(lg * (idx + 1.0))[:, None], (LBLK, LANES)))
        kdec.append(np.broadcast_to(np.exp(lg * (LBLK - 1.0 - idx))[:, None], (LBLK, LANES)))
        cdec.append(math.exp(lg * LBLK))
    f = lambda a: jnp.asarray(np.stack(a), dtype=F32)
    return f(dmask), f(qdec), f(kdec), tuple(cdec)


def _gla_tables():
    n = np.arange(LBLK)
    mats = [n[None, :] <= n[:, None]]
    for p in range(N_FINE):
        ref = (n >> (p + 1) << (p + 1)) + (1 << p) - 1
        lo, hi = np.minimum(n, ref), np.maximum(n, ref)
        mats.append((n[None, :] > lo[:, None]) & (n[None, :] <= hi[:, None]))
    cum = np.concatenate(mats, axis=0).astype(np.float32)
    cum = np.concatenate([cum, cum], axis=1)
    m = np.arange(GRP)
    x = m[:, None] ^ m[None, :]
    lvl = np.where(x == 0, -1, np.floor(np.log2(np.maximum(x, 1))).astype(np.int64))
    hidden = (m[None, :] // CHUNK) > (m[:, None] // CHUNK)
    lvl = np.where(hidden, -2, lvl)
    second = ((n[None, :] >> np.arange(N_FINE, N_LEVELS)[:, None]) & 1).astype(np.float64)
    sgn = (2.0 * second - 1.0) * math.log2(math.e)
    sgn = np.broadcast_to(sgn[:, :, None], (N_LEVELS - N_FINE, LBLK, LANES))
    tri = np.where(m[None, :] <= m[:, None], 1, np.where(hidden, 0, 2))
    return (jnp.asarray(cum, dtype=BF16), jnp.asarray(lvl, dtype=jnp.int32),
            jnp.asarray(sgn, dtype=F32), jnp.asarray(tri, dtype=jnp.int32))


def _split_ref_rows(g, half):
    n = g.shape[0]
    size = 2 * half
    g3 = g.reshape(n // size, size, g.shape[1])
    ref = jnp.broadcast_to(g3[:, half - 1:half, :], g3.shape)
    return ref.reshape(g.shape)


GLA_MILD_DECAY = 60.0


def _ret_block(q_ref, k_ref, v_ref, rows, dmask_ref, qdec_ref, kdec_ref, cdec, state_ref, o_ref, slot):
    for h in range(HEADS):
        qk_cols = slice(h * DK, (h + 1) * DK)
        v_cols = slice(h * DV, (h + 1) * DV)
        q = q_ref[rows, qk_cols]
        k = k_ref[rows, qk_cols]
        v = v_ref[rows, v_cols]
        st = state_ref[h]
        s = _dot_nt(q, k) * dmask_ref[h]
        qdec = qdec_ref[h]
        inter = _dot(q, st.astype(BF16)) * jnp.concatenate([qdec, qdec], axis=1)
        o = _dot(s.astype(BF16), v) + inter
        kd = (k.astype(F32) * kdec_ref[h]).astype(BF16)
        state_ref[h] = cdec[h] * st + _dot_tn(kd, v)
        mu = jnp.mean(o, axis=-1, keepdims=True)
        oc = o - mu
        var = jnp.mean(oc * oc, axis=-1, keepdims=True)
        o_ref[slot, rows, v_cols] = (oc * lax.rsqrt(var + NORM_EPS)).astype(BF16)


def _gla_cum(la_ref, rows, cum_ref):
    la = la_ref[rows, :]
    la_hi = la.astype(BF16)
    la_lo = (la - la_hi.astype(F32)).astype(BF16)
    la2 = jnp.concatenate([la_hi, la_lo], axis=0)
    return la2, _dot(cum_ref[0:LBLK, :], la2)


def _gla_finish(h, q, k, v, g, s_top, s_bot_row, state_ref, nw_ref, o_ref, slot, rows):
    top = slice(0, GRP)
    v_cols = slice(h * DV, (h + 1) * DV)
    st = state_ref[h]
    inter = _dot_nt((q * jnp.exp(g)).astype(BF16), st.astype(BF16))
    o_top = _dot(s_top.astype(BF16), v[top])
    o_bot = _dot(s_bot_row.astype(BF16), v)
    o = jnp.concatenate([o_top, o_bot], axis=0) + inter
    g_last = g[LBLK - 1:LBLK, :]
    kd = (k * jnp.exp(g_last - g)).astype(BF16)
    state_ref[h] = st * jnp.exp(g_last) + _dot_tn(v, kd)
    ms = jnp.mean(o * o, axis=-1, keepdims=True)
    o_ref[slot, rows, v_cols] = (o * lax.rsqrt(ms + NORM_EPS) * nw_ref[...]).astype(BF16)


def _gla_block(q_ref, k_ref, v_ref, rows, la2, g_all, cum_ref, lvl_ref, sgn_ref, nw_ref, state_ref,
               o_ref, slot):
    top, bot = slice(0, GRP), slice(GRP, LBLK)
    lvl = lvl_ref[...]
    fine = _dot(cum_ref[LBLK:, :], la2)
    for h in range(HEADS):
        qk_cols = slice(h * DK, (h + 1) * DK)
        g = g_all[:, qk_cols]
        qb = q_ref[rows, qk_cols]
        kb = k_ref[rows, qk_cols]
        v = v_ref[rows, h * DV:(h + 1) * DV]
        q = qb.astype(F32)
        k = kb.astype(F32)
        s_grp = [jnp.where(lvl == -1, _dot_nt(qb[r], kb[r]), 0.0) for r in (top, bot)]
        s_cross = None
        for p in range(N_LEVELS):
            if p < N_FINE:
                e = jnp.exp(fine[p * LBLK:(p + 1) * LBLK, qk_cols])
            else:
                e = jnp.exp2((g - _split_ref_rows(g, 1 << p)) * sgn_ref[p - N_FINE])
            qe = (q * e).astype(BF16)
            ke = (k * e).astype(BF16)
            if (2 << p) <= GRP:
                s_grp = [jnp.where(lvl == p, _dot_nt(qe[r], ke[r]), s)
                         for r, s in zip((top, bot), s_grp)]
            else:
                s_cross = _dot_nt(qe[bot], ke[top])
        _gla_finish(h, q, k, v, g, s_grp[0], jnp.concatenate([s_cross, s_grp[1]], axis=1),
                    state_ref, nw_ref, o_ref, slot, rows)


def _gla_block_mild(q_ref, k_ref, v_ref, rows, g_all, tri_ref, nw_ref, state_ref, o_ref, slot):
    top, bot = slice(0, GRP), slice(GRP, LBLK)
    tri = tri_ref[...]
    for h in range(HEADS):
        qk_cols = slice(h * DK, (h + 1) * DK)
        g = g_all[:, qk_cols]
        q = q_ref[rows, qk_cols].astype(F32)
        k = k_ref[rows, qk_cols].astype(F32)
        v = v_ref[rows, h * DV:(h + 1) * DV]
        g0 = g[0:1, :]
        shrink = jnp.exp(g - g0)
        grow = jnp.exp(g0 - g)
        q_s, q_g = (q * shrink).astype(BF16), (q * grow).astype(BF16)
        k_s, k_g = (k * shrink).astype(BF16), (k * grow).astype(BF16)

        def group_scores(r):
            after = _dot_nt(q_s[r], k_g[r])
            before = _dot_nt(q_g[r], k_s[r])
            return jnp.where(tri == 1, after, jnp.where(tri == 2, before, 0.0))

        s_bot_row = jnp.concatenate([_dot_nt(q_s[bot], k_g[top]), group_scores(bot)], axis=1)
        _gla_finish(h, q, k, v, g, group_scores(top), s_bot_row, state_ref, nw_ref, o_ref, slot, rows)


def _mix_kernel(qa_ref, ka_ref, va_ref, qb_ref, kb_ref, vb_ref, la_ref, h_ref, nw_ref,
                wra, wrb, wga, wgb, wa, wb, wo,
                dmask_ref, qdec_ref, kdec_ref, cum_ref, lvl_ref, sgn_ref, tri_ref, gnw_ref,
                o_ref, oa_scr, ob_scr, ret_state, gla_state, *, cdec, blocks_per_seq):
    s = pl.program_id(0)
    blocks = [slice(i * LBLK, (i + 1) * LBLK) for i in range(TBLK // LBLK)]
    slabs = [slice(r0, r0 + SLAB) for r0 in range(0, TBLK, SLAB)]

    @pl.when(lax.rem(s, blocks_per_seq) == 0)
    def _():
        ret_state[...] = jnp.zeros_like(ret_state)
        gla_state[...] = jnp.zeros_like(gla_state)

    cums = [_gla_cum(la_ref, rows, cum_ref) for rows in blocks]
    decay = [jnp.max(g[0:1, :] - g[LBLK - 1:LBLK, :]) for _, g in cums]
    mild = functools.reduce(jnp.maximum, decay) < GLA_MILD_DECAY

    for rows in blocks:
        _ret_block(qa_ref, ka_ref, va_ref, rows, dmask_ref, qdec_ref, kdec_ref, cdec, ret_state, oa_scr, 0)

    early = []
    for rows in slabs:
        h = h_ref[rows, :]
        u = _rms(h, nw_ref[...]).astype(BF16)
        r_a = _dot(u, wra[...])
        og_a = (r_a * _sigmoid(r_a) * oa_scr[0, rows, :].astype(F32)).astype(BF16)
        gated_a = _sigmoid(_dot(u, wga[...])) * _dot(og_a, wa[...])
        r_b = _dot(u, wrb[...])
        early.append((h, gated_a, r_b * _sigmoid(r_b), _sigmoid(_dot(u, wgb[...]))))

    @pl.when(mild)
    def _():
        for rows, (_, g) in zip(blocks, cums):
            _gla_block_mild(qb_ref, kb_ref, vb_ref, rows, g, tri_ref, gnw_ref, gla_state, ob_scr, 0)

    @pl.when(jnp.logical_not(mild))
    def _():
        for rows, (la2, g) in zip(blocks, cums):
            _gla_block(qb_ref, kb_ref, vb_ref, rows, la2, g, cum_ref, lvl_ref, sgn_ref, gnw_ref, gla_state,
                       ob_scr, 0)

    for rows, (h, gated_a, silu_b, gate_b) in zip(slabs, early):
        og_b = (silu_b * ob_scr[0, rows, :].astype(F32)).astype(BF16)
        merged = gated_a + gate_b * _dot(og_b, wb[...])
        o_ref[rows, :] = h + _dot(merged.astype(BF16), wo[...])


def _mix(qa, ka, va, qb, kb, vb, la, h1, nw, wra, wrb, wga, wgb, wa, wb, wo, gla_nw, seq):
    t = h1.shape[0]
    dmask, qdec, kdec, cdec = _ret_tables()
    cum, lvl, sgn, tri = _gla_tables()

    def tile(w):
        return pl.BlockSpec((TBLK, w), lambda s: (s, 0))

    sq = _const_spec((D_MODEL, D_MODEL))
    return pl.pallas_call(
        functools.partial(_mix_kernel, cdec=cdec, blocks_per_seq=seq // TBLK),
        grid=(t // TBLK,),
        in_specs=[tile(QK), tile(QK), tile(VW), tile(QK), tile(QK), tile(VW), tile(QK), tile(D_MODEL),
                  _const_spec((1, D_MODEL)), sq, sq, sq, sq, sq, sq, sq,
                  _const_spec((HEADS, LBLK, LBLK)), _const_spec((HEADS, LBLK, LANES)),
                  _const_spec((HEADS, LBLK, LANES)), _const_spec(((1 + N_FINE) * LBLK, 2 * LBLK)),
                  _const_spec((GRP, GRP)), _const_spec((N_LEVELS - N_FINE, LBLK, LANES)),
                  _const_spec((GRP, GRP)), _const_spec((1, DV))],
        out_specs=tile(D_MODEL),
        out_shape=jax.ShapeDtypeStruct((t, D_MODEL), F32),
        scratch_shapes=[pltpu.VMEM((1, TBLK, VW), BF16), pltpu.VMEM((1, TBLK, VW), BF16),
                        pltpu.VMEM((HEADS, DK, DV), F32), pltpu.VMEM((HEADS, DV, DK), F32)],
        compiler_params=pltpu.CompilerParams(
            dimension_semantics=("arbitrary",), vmem_limit_bytes=MIX_VMEM_LIMIT),
        name="mix",
    )(qa, ka, va, qb, kb, vb, la, h1, nw, wra, wrb, wga, wgb, wa, wb, wo,
      dmask, qdec, kdec, cum, lvl, sgn, tri, gla_nw)


def _rope_tables(seq):
    half = DK // 2
    inv_freq = ROPE_BASE ** (-np.arange(half, dtype=np.float64) / half)
    ang = np.arange(seq, dtype=np.float64)[:, None] * inv_freq[None, :]
    cos = np.concatenate([np.cos(ang), np.cos(ang)], axis=1)
    sin = np.concatenate([-np.sin(ang), np.sin(ang)], axis=1)
    return jnp.asarray(cos, dtype=F32), jnp.asarray(sin, dtype=F32)


def kernel(x, norm_ffn1, ffn1_w1, ffn1_w3, ffn1_w2, norm_mix, w_in, gla_gate_w2, gla_gate_b, gla_norm_w,
           w_branch_ret, w_branch_gla, w_out, norm_ffn2, ffn2_w1, ffn2_w3, ffn2_w2, norm_final):
    batch, seq, _ = x.shape
    depth = norm_ffn1.shape[0]
    t = batch * seq
    h = x.reshape(t, D_MODEL)
    cos_t, sin_t = _rope_tables(seq)
    row = lambda a: a.reshape(1, -1).astype(F32)
    bf = lambda a: a.astype(BF16)
    assert depth == 1, "the block's final RMSNorm is fused into the layer's closing FFN call"
    for l in range(depth):
        cols, off = [], 0
        for width in IN_SPLITS:
            cols.append(w_in[l][:, off:off + width])
            off += width
        wqa, wka, wva, wra, wqb, wkb, wvb, wrb, wlr, wga, wgb = cols
        wlr_p = jnp.pad(wlr, ((0, 0), (0, LANES - GATE_RANK))).astype(BF16)
        w2h = gla_gate_w2[l].astype(BF16)
        w2l = (gla_gate_w2[l] - w2h.astype(F32)).astype(BF16)
        w2p = jnp.pad(jnp.concatenate([w2h, w2h, w2l], axis=0), ((0, LANES - 3 * GATE_RANK), (0, 0)))
        proj_w = (bf(wqa), bf(wka), bf(wva), bf(wqb), bf(wkb), bf(wvb), wlr_p, w2p)
        h, qa, ka, va, qb, kb, vb, la = _ffn_proj(
            h, row(norm_ffn1[l]), bf(ffn1_w1[l]), bf(ffn1_w3[l]), bf(ffn1_w2[l]), row(norm_mix[l]),
            proj_w, row(gla_gate_b[l]), cos_t, sin_t, seq)
        h = _mix(qa, ka, va, qb, kb, vb, la, h, row(norm_mix[l]), bf(wra), bf(wrb), bf(wga), bf(wgb),
                 bf(w_branch_ret[l]), bf(w_branch_gla[l]), bf(w_out[l]), row(gla_norm_w[l]), seq)
        h = _ffn_final(h, row(norm_ffn2[l]), bf(ffn2_w1[l]), bf(ffn2_w3[l]), bf(ffn2_w2[l]), row(norm_final))
    return h.reshape(batch, seq, D_MODEL)
```
